```python
import functools
import jax, jax.numpy as jnp
from jax import lax
import numpy as np

D_MODEL = 2048
BATCH = 32
SEQ = 256
DEPTH = 2
DEC_BATCH = 2
DEC_SEQ = 4096
PAST_LEN = 256

GRID_W = 64
N_ATTN_LAYERS = (DEPTH + 1) // 2
N_CMLP_LAYERS = DEPTH // 2
D_CONV = D_MODEL // 2
CONV_WIDTH = 31
CONV_PAD = CONV_WIDTH // 2
MLA_HEADS = D_MODEL // 256
Q_RANK = D_MODEL // 4
KV_RANK = D_MODEL // 8
NOPE_DIM = 128
ROPE_DIM = 64
V_DIM = 128
QK_DIM = NOPE_DIM + ROPE_DIM
ROPE_THETA = 10000.0
Q_BLOCK = 128
D_CMLP = D_MODEL
CHUNK = 128
CMLP_GROUPS = 8
D_FF = 256 * (-(-8 * D_MODEL // (3 * 256)))
EVEN_IN = 2 * D_CONV + Q_RANK + KV_RANK + ROPE_DIM
EVEN_OUT = D_CONV + MLA_HEADS * V_DIM
ATTN_SCALE = QK_DIM ** -0.5
EPS = 1e-6

kernel_name = 'hybrid_flow_conv_mla_gmlp_step'


def rmsnorm(x, g):
    xf = x.astype(jnp.float32)
    y = xf * lax.rsqrt(jnp.mean(xf * xf, axis=-1, keepdims=True) + EPS)
    return (y * g.astype(jnp.float32)).astype(x.dtype)


def layernorm(x, g, b):
    xf = x.astype(jnp.float32)
    mu = jnp.mean(xf, axis=-1, keepdims=True)
    var = jnp.mean(jnp.square(xf - mu), axis=-1, keepdims=True)
    y = (xf - mu) * lax.rsqrt(var + EPS)
    return (y * g.astype(jnp.float32) + b.astype(jnp.float32)).astype(x.dtype)


def adaln(cond, w, b):
    m = jax.nn.silu(cond) @ w + b
    return jnp.split(m[:, None, :], 6, axis=-1)


def swiglu(h, wg, wu, wd):
    return (jax.nn.silu(h @ wg) * (h @ wu)) @ wd


def axial_rope_tables(n_tokens, dtype):
    n_rows = n_tokens // GRID_W
    rows = jnp.repeat(jnp.arange(n_rows), GRID_W)
    cols = jnp.tile(jnp.arange(GRID_W), n_rows)
    pos = jnp.stack([rows, cols], axis=-1).astype(jnp.float32)
    n_freq = ROPE_DIM // 4
    inv_freq = jnp.power(ROPE_THETA, -jnp.arange(n_freq, dtype=jnp.float32) * 2.0 / (ROPE_DIM // 2))
    ang = pos[:, :, None] * inv_freq
    return jnp.cos(ang)[:, None].astype(dtype), jnp.sin(ang)[:, None].astype(dtype)


def apply_axial_rope(x, cos, sin):
    xr = x.reshape(x.shape[:-1] + (2, 2, ROPE_DIM // 4))
    x1, x2 = xr[..., 0, :], xr[..., 1, :]
    out = jnp.stack([x1 * cos - x2 * sin, x1 * sin + x2 * cos], axis=-2)
    return out.reshape(x.shape)


def block_attention(q, k, v):
    b, t, h, d = q.shape
    qb = q.reshape(b, t // Q_BLOCK, Q_BLOCK, h, d).transpose(1, 0, 2, 3, 4)

    def one_block(qi):
        s = jnp.einsum('bqhd,bkhd->bhqk', qi, k).astype(jnp.float32) * ATTN_SCALE
        p = jax.nn.softmax(s, axis=-1).astype(v.dtype)
        return jnp.einsum('bhqk,bkhd->bqhd', p, v)

    o = lax.map(one_block, qb)
    return o.transpose(1, 0, 2, 3, 4).reshape(b, t, h, v.shape[-1])


def conformer_conv(a_in, conv_w, conv_b, ln_g, ln_b):
    a = a_in[..., :D_CONV] * jax.nn.sigmoid(a_in[..., D_CONV:])
    a = lax.conv_general_dilated(a, conv_w, window_strides=(1,), padding=[(CONV_PAD, CONV_PAD)],
                                 dimension_numbers=('NWC', 'WIO', 'NWC'),
                                 feature_group_count=D_CONV) + conv_b
    return jax.nn.silu(layernorm(a, ln_g, ln_b))


def mla(q_c, kv_c, k_pe, g_qa, w_qb, g_kva, w_kvb, rope, ctx):
    b, t, _ = q_c.shape
    q = (rmsnorm(q_c, g_qa) @ w_qb).reshape(b, t, MLA_HEADS, QK_DIM)
    q_nope, q_pe = q[..., :NOPE_DIM], q[..., NOPE_DIM:]
    ckv = rmsnorm(kv_c, g_kva)
    k_pe = k_pe[:, :, None, :]
    if rope is not None:
        cos, sin = rope
        q_pe = apply_axial_rope(q_pe, cos, sin)
        k_pe = apply_axial_rope(k_pe, cos, sin)
    ckv_all, kpe_all = ckv, k_pe
    if ctx is not None:
        ctx_ckv, ctx_kpe = ctx
        ckv_all = jnp.concatenate([ctx_ckv, ckv], axis=1)
        kpe_all = jnp.concatenate([ctx_kpe[:, :, None, :], k_pe], axis=1)
    tk = ckv_all.shape[1]
    kv = (ckv_all @ w_kvb).reshape(b, tk, MLA_HEADS, NOPE_DIM + V_DIM)
    k = jnp.concatenate([kv[..., :NOPE_DIM],
                         jnp.broadcast_to(kpe_all, (b, tk, MLA_HEADS, ROPE_DIM))], axis=-1)
    v = kv[..., NOPE_DIM:]
    o = block_attention(jnp.concatenate([q_nope, q_pe], axis=-1), k, v)
    return o.reshape(b, t, MLA_HEADS * V_DIM), ckv, k_pe[:, :, 0, :]


def even_mixer(h, p, rope, ctx):
    w_in, conv_w, conv_b, conv_ln_g, conv_ln_b, g_qa, w_qb, g_kva, w_kvb, w_o = p
    z = h @ w_in
    o1 = 2 * D_CONV
    o2 = o1 + Q_RANK
    o3 = o2 + KV_RANK
    a = conformer_conv(z[..., :o1], conv_w, conv_b, conv_ln_g, conv_ln_b)
    att, ckv, kpe = mla(z[..., o1:o2], z[..., o2:o3], z[..., o3:], g_qa, w_qb, g_kva, w_kvb, rope, ctx)
    return jnp.concatenate([a, att], axis=-1) @ w_o, (ckv, kpe)


def cmlp_mixer(h, p):
    w_in, ln_g, ln_b, w_s, b_s, w_o = p
    b, t, _ = h.shape
    z = jax.nn.gelu(h @ w_in)
    u, v = z[..., :D_CMLP], z[..., D_CMLP:]
    v = layernorm(v, ln_g, ln_b).reshape(b, t // CHUNK, CHUNK, CMLP_GROUPS, D_CMLP // CMLP_GROUPS)
    v = jnp.einsum('gpq,bnqgc->bnpgc', w_s, v) + b_s.T[:, :, None]
    return (u * v.reshape(b, t, D_CMLP)) @ w_o, None


def residual_block(x, mods, shared, mixer):
    sh_m, sc_m, gt_m, sh_f, sc_f, gt_f = mods
    g_mix, g_ffn, wg, wu, wd = shared
    out, aux = mixer(rmsnorm(x, g_mix) * (1 + sc_m) + sh_m)
    x = x + gt_m * out
    x = x + gt_f * swiglu(rmsnorm(x, g_ffn) * (1 + sc_f) + sh_f, wg, wu, wd)
    return x, aux


def setup_inputs(seed: int = 0) -> dict:
    key = jax.random.key(seed)
    ks = iter(jax.random.split(key, 40))

    def nrm(shape, scale):
        return jax.random.normal(next(ks), shape, jnp.float32) * scale

    def gain(shape):
        return 1.0 + nrm(shape, 0.02)

    NE, NO = N_ATTN_LAYERS, N_CMLP_LAYERS
    return {
        'x_prompt': nrm((BATCH, SEQ, D_MODEL), 1.0),
        'x_sample': nrm((DEC_BATCH, DEC_SEQ, D_MODEL), 1.0),
        'cache_ckv': nrm((DEC_BATCH, NE, PAST_LEN, KV_RANK), 1.0),
        'cache_kpe': nrm((DEC_BATCH, NE, PAST_LEN, ROPE_DIM), 1.0),
        'c': nrm((DEC_BATCH, D_MODEL), 1.0),
        'c_ctx': nrm((D_MODEL,), 1.0),
        'mod_w': nrm((DEPTH, D_MODEL, 6 * D_MODEL), 0.5 * D_MODEL ** -0.5),
        'mod_b': nrm((DEPTH, 6 * D_MODEL), 0.02),
        'norm_mix_g': gain((DEPTH, D_MODEL)),
        'norm_ffn_g': gain((DEPTH, D_MODEL)),
        'ffn_w_gate': nrm((DEPTH, D_MODEL, D_FF), D_MODEL ** -0.5),
        'ffn_w_up': nrm((DEPTH, D_MODEL, D_FF), D_MODEL ** -0.5),
        'ffn_w_down': nrm((DEPTH, D_FF, D_MODEL), D_FF ** -0.5),
        'ev_w_in': nrm((NE, D_MODEL, EVEN_IN), D_MODEL ** -0.5),
        'ev_conv_w': nrm((NE, CONV_WIDTH, 1, D_CONV), CONV_WIDTH ** -0.5),
        'ev_conv_b': nrm((NE, D_CONV), 0.02),
        'ev_conv_ln_g': gain((NE, D_CONV)),
        'ev_conv_ln_b': nrm((NE, D_CONV), 0.02),
        'ev_q_norm_g': gain((NE, Q_RANK)),
        'ev_w_qb': nrm((NE, Q_RANK, MLA_HEADS * QK_DIM), Q_RANK ** -0.5),
        'ev_kv_norm_g': gain((NE, KV_RANK)),
        'ev_w_kvb': nrm((NE, KV_RANK, MLA_HEADS * (NOPE_DIM + V_DIM)), KV_RANK ** -0.5),
        'ev_w_o': nrm((NE, EVEN_OUT, D_MODEL), EVEN_OUT ** -0.5),
        'od_w_in': nrm((NO, D_MODEL, 2 * D_CMLP), D_MODEL ** -0.5),
        'od_ln_g': gain((NO, D_CMLP)),
        'od_ln_b': nrm((NO, D_CMLP), 0.02),
        'od_w_s': nrm((NO, CMLP_GROUPS, CHUNK, CHUNK), CHUNK ** -0.5),
        'od_b_s': gain((NO, CMLP_GROUPS, CHUNK)),
        'od_w_o': nrm((NO, D_CMLP, D_MODEL), D_CMLP ** -0.5),
        'final_norm_g': gain((D_MODEL,)),
    }


def reference(x_prompt, x_sample, cache_ckv, cache_kpe, c, c_ctx,
              mod_w, mod_b, norm_mix_g, norm_ffn_g, ffn_w_gate, ffn_w_up, ffn_w_down,
              ev_w_in, ev_conv_w, ev_conv_b, ev_conv_ln_g, ev_conv_ln_b,
              ev_q_norm_g, ev_w_qb, ev_kv_norm_g, ev_w_kvb, ev_w_o,
              od_w_in, od_ln_g, od_ln_b, od_w_s, od_b_s, od_w_o, final_norm_g):
    rope = axial_rope_tables(x_sample.shape[1], x_sample.dtype)
    xp, xs = x_prompt, x_sample
    new_ckv, new_kpe = [], []
    for l in range(DEPTH):
        mods_p = adaln(c_ctx[None, :], mod_w[l], mod_b[l])
        mods_s = adaln(c, mod_w[l], mod_b[l])
        shared = (norm_mix_g[l], norm_ffn_g[l], ffn_w_gate[l], ffn_w_up[l], ffn_w_down[l])
        if l % 2 == 0:
            e = l // 2
            ev = (ev_w_in[e], ev_conv_w[e], ev_conv_b[e], ev_conv_ln_g[e], ev_conv_ln_b[e],
                  ev_q_norm_g[e], ev_w_qb[e], ev_kv_norm_g[e], ev_w_kvb[e], ev_w_o[e])
            xp, ctx_kv = residual_block(xp, mods_p, shared,
                                        functools.partial(even_mixer, p=ev, rope=None, ctx=None))
            new_ckv.append(ctx_kv[0])
            new_kpe.append(ctx_kv[1])
            xs, _ = residual_block(xs, mods_s, shared,
                                   functools.partial(even_mixer, p=ev, rope=rope,
                                                     ctx=(cache_ckv[:, e], cache_kpe[:, e])))
        else:
            o = l // 2
            od = (od_w_in[o], od_ln_g[o], od_ln_b[o], od_w_s[o], od_b_s[o], od_w_o[o])
            xp, _ = residual_block(xp, mods_p, shared, functools.partial(cmlp_mixer, p=od))
            xs, _ = residual_block(xs, mods_s, shared, functools.partial(cmlp_mixer, p=od))
    y_prompt = rmsnorm(xp, final_norm_g)
    y_sample = rmsnorm(xs, final_norm_g)
    state_ckv = jnp.stack(new_ckv, axis=1)
    state_kpe = jnp.stack(new_kpe, axis=1)
    return (y_prompt, y_sample, state_ckv, state_kpe)
```

```python
import functools

import jax
import jax.numpy as jnp
from jax import lax
from jax.experimental import pallas as pl
from jax.experimental.pallas import tpu as pltpu

D_MODEL = 2048
DEPTH = 2
GRID_W = 64
D_CONV = D_MODEL // 2
CONV_WIDTH = 31
CONV_PAD = CONV_WIDTH // 2
MLA_HEADS = D_MODEL // 256
Q_RANK = D_MODEL // 4
KV_RANK = D_MODEL // 8
NOPE_DIM = 128
ROPE_DIM = 64
V_DIM = 128
QK_DIM = NOPE_DIM + ROPE_DIM
ROPE_THETA = 10000.0
D_CMLP = D_MODEL
CHUNK = 128
CMLP_GROUPS = 8
CMLP_GROUP_W = D_CMLP // CMLP_GROUPS
D_FF = 256 * (-(-8 * D_MODEL // (3 * 256)))
ATTN_SCALE = QK_DIM ** -0.5
EPS = 1e-6

LANES = 128
SUBLANES = 8
HEAD_PAD = 2 * LANES
CONV_HALO = 2 * SUBLANES
N_MODS = 8
MIB = 1024 * 1024

BF16 = jnp.bfloat16
F32 = jnp.float32


def _cparams(semantics, vmem_mib):
    return pltpu.CompilerParams(dimension_semantics=semantics, vmem_limit_bytes=vmem_mib * MIB)


def _silu(x):
    return x * (1.0 / (1.0 + jnp.exp(-x)))


def _gelu_tanh(x):
    cdf = 0.5 * (1.0 + jnp.tanh(0.7978845608028654 * (x + 0.044715 * (x * x * x))))
    return x * cdf


def _rms(x, g):
    return (x * lax.rsqrt(jnp.mean(x * x, axis=-1, keepdims=True) + EPS)) * g


def _layernorm(x, g, b):
    mu = jnp.mean(x, axis=-1, keepdims=True)
    xc = x - mu
    var = jnp.mean(xc * xc, axis=-1, keepdims=True)
    return xc * lax.rsqrt(var + EPS) * g + b


def _modnorm(x, g, mods, row):
    return _rms(x, g) * (1.0 + mods[row + 1:row + 2, :]) + mods[row:row + 1, :]


def _dot(a, b):
    return jnp.dot(a, b, preferred_element_type=F32)


def _mods_kernel(cond_ref, w_ref, b_ref, o_ref):
    s = _silu(cond_ref[...]).astype(BF16)
    o_ref[0] = _dot(s, w_ref[0].astype(BF16)) + b_ref[0]


def _mods_call(cond8, mod_w, mod_b):
    tn = 1024
    n = mod_w.shape[-1]
    return pl.pallas_call(
        _mods_kernel,
        grid=(DEPTH, n // tn),
        in_specs=[
            pl.BlockSpec((N_MODS, D_MODEL), lambda l, j: (0, 0)),
            pl.BlockSpec((1, D_MODEL, tn), lambda l, j: (l, 0, j)),
            pl.BlockSpec((1, 1, tn), lambda l, j: (l, 0, j)),
        ],
        out_specs=pl.BlockSpec((1, N_MODS, tn), lambda l, j: (l, 0, j)),
        out_shape=jax.ShapeDtypeStruct((DEPTH, N_MODS, n), F32),
        compiler_params=_cparams(("arbitrary", "arbitrary"), 32),
        name="mods",
    )(cond8, mod_w, mod_b.reshape(DEPTH, 1, n))


O_GLU = 2 * D_CONV
O_Q = O_GLU + Q_RANK
O_KV = O_Q + KV_RANK
O_END = O_KV + 2 * ROPE_DIM


def _even_in_kernel(rope, x_ref, mods_ref, g_ref, w_ref, gq_ref, gkv_ref, wq_ref, *rest):
    if rope:
        wqs_ref, cq_ref, sq_ref, ck_ref, sk_ref, a_ref, q_ref, ckv_ref, kpe_ref = rest
    else:
        a_ref, q_ref, ckv_ref, kpe_ref = rest
    h = _modnorm(x_ref[...], g_ref[...], mods_ref[0], 0).astype(BF16)
    lin = _dot(h, w_ref[:, :D_CONV])
    gate = _dot(h, w_ref[:, D_CONV:O_GLU])
    a_ref[...] = lin * (1.0 / (1.0 + jnp.exp(-gate)))
    qn = _rms(_dot(h, w_ref[:, O_GLU:O_Q]), gq_ref[...]).astype(BF16)
    ckv_ref[...] = _rms(_dot(h, w_ref[:, O_Q:O_KV]), gkv_ref[...])
    kp = _dot(h, w_ref[:, O_KV:O_END])
    q = _dot(qn, wq_ref[...])
    if rope:
        kpe_ref[...] = kp[:, :ROPE_DIM] * ck_ref[...] + kp[:, ROPE_DIM:] * sk_ref[...]
        qs = _dot(qn, wqs_ref[...])
        cq = cq_ref[...]
        sq = sq_ref[...]
        for hd in range(MLA_HEADS):
            c0 = hd * HEAD_PAD
            q_ref[:, c0:c0 + LANES] = q[:, c0:c0 + LANES].astype(BF16)
            q_ref[:, c0 + LANES:c0 + HEAD_PAD] = (
                q[:, c0 + LANES:c0 + HEAD_PAD] * cq + qs[:, hd * LANES:(hd + 1) * LANES] * sq).astype(BF16)
    else:
        kpe_ref[...] = kp[:, :ROPE_DIM]
        q_ref[...] = q.astype(BF16)


def _even_in_call(x, mods, rows_per_cond, g_mix, w1, g_qa, g_kva, wq, rope_args, tm=256):
    r = x.shape[0]
    rope = rope_args is not None
    const = lambda i: (0, 0)
    row = lambda i: (i, 0)
    in_specs = [
        pl.BlockSpec((tm, D_MODEL), row),
        pl.BlockSpec((1, N_MODS, D_MODEL), lambda i: ((i * tm) // rows_per_cond, 0, 0)),
        pl.BlockSpec((1, D_MODEL), const),
        pl.BlockSpec(w1.shape, const),
        pl.BlockSpec((1, Q_RANK), const),
        pl.BlockSpec((1, KV_RANK), const),
        pl.BlockSpec(wq.shape, const),
    ]
    args = [x, mods, g_mix, w1, g_qa, g_kva, wq]
    if rope:
        wqs, cq, sq, ck, sk = rope_args
        t_blocks = cq.shape[0] // tm
        pos = lambda i: (i % t_blocks, 0)
        in_specs += [
            pl.BlockSpec(wqs.shape, const),
            pl.BlockSpec((tm, LANES), pos),
            pl.BlockSpec((tm, LANES), pos),
            pl.BlockSpec((tm, ROPE_DIM), pos),
            pl.BlockSpec((tm, ROPE_DIM), pos),
        ]
        args += [wqs, cq, sq, ck, sk]
    return pl.pallas_call(
        functools.partial(_even_in_kernel, rope),
        grid=(r // tm,),
        in_specs=in_specs,
        out_specs=[
            pl.BlockSpec((tm, D_CONV), row),
            pl.BlockSpec((tm, MLA_HEADS * HEAD_PAD), row),
            pl.BlockSpec((tm, KV_RANK), row),
            pl.BlockSpec((tm, ROPE_DIM), row),
        ],
        out_shape=[
            jax.ShapeDtypeStruct((r, D_CONV), F32),
            jax.ShapeDtypeStruct((r, MLA_HEADS * HEAD_PAD), BF16),
            jax.ShapeDtypeStruct((r, KV_RANK), F32),
            jax.ShapeDtypeStruct((r, ROPE_DIM), F32),
        ],
        compiler_params=_cparams(("arbitrary",), 56),
        name="even_in",
    )(*args)


CONV_ROWS = 64


def _conv_kernel(tt, n_t, cur_ref, prev_ref, next_ref, w_ref, b_ref, g_ref, beta_ref, o_ref, buf_ref, y_ref):
    i = pl.program_id(1)
    keep_prev = (i > 0).astype(F32)
    keep_next = (i < n_t - 1).astype(F32)
    buf_ref[0:CONV_HALO, :] = prev_ref[0] * keep_prev
    buf_ref[CONV_HALO:CONV_HALO + tt, :] = cur_ref[0]
    buf_ref[CONV_HALO + tt:, :] = next_ref[0] * keep_next
    first = CONV_HALO - CONV_PAD
    for c in range(D_CONV // LANES):
        cs = slice(c * LANES, (c + 1) * LANES)
        for r0 in range(0, tt, CONV_ROWS):
            acc = jnp.zeros((CONV_ROWS, LANES), F32)
            for k in range(CONV_WIDTH):
                acc = acc + buf_ref[r0 + first + k:r0 + first + k + CONV_ROWS, cs] * w_ref[k:k + 1, cs]
            y_ref[r0:r0 + CONV_ROWS, cs] = acc + b_ref[:, cs]
    o_ref[0] = _silu(_layernorm(y_ref[...], g_ref[...], beta_ref[...])).astype(BF16)


def _conv_call(a, batch, t, conv_w, conv_b, ln_g, ln_b, tt):
    n_t = t // tt
    a3 = a.reshape(batch, t, D_CONV)
    hb = tt // CONV_HALO
    n_hb = t // CONV_HALO
    const = lambda b, i: (0, 0)
    out = pl.pallas_call(
        functools.partial(_conv_kernel, tt, n_t),
        grid=(batch, n_t),
        in_specs=[
            pl.BlockSpec((1, tt, D_CONV), lambda b, i: (b, i, 0)),
            pl.BlockSpec((1, CONV_HALO, D_CONV), lambda b, i: (b, jnp.maximum(i * hb - 1, 0), 0)),
            pl.BlockSpec((1, CONV_HALO, D_CONV), lambda b, i: (b, jnp.minimum((i + 1) * hb, n_hb - 1), 0)),
            pl.BlockSpec((CONV_WIDTH, D_CONV), const),
            pl.BlockSpec((1, D_CONV), const),
            pl.BlockSpec((1, D_CONV), const),
            pl.BlockSpec((1, D_CONV), const),
        ],
        out_specs=pl.BlockSpec((1, tt, D_CONV), lambda b, i: (b, i, 0)),
        out_shape=jax.ShapeDtypeStruct((batch, t, D_CONV), BF16),
        scratch_shapes=[
            pltpu.VMEM((tt + 2 * CONV_HALO, D_CONV), F32),
            pltpu.VMEM((tt, D_CONV), F32),
        ],
        compiler_params=_cparams(("arbitrary", "arbitrary"), 32),
        name="conv",
    )(a3, a3, a3, conv_w, conv_b, ln_g, ln_b)
    return out.reshape(batch * t, D_CONV)


KV_ROWS = 1024


def _fill_kv(k_ref, v_ref, row0, ckv, kpe, wkv):
    kv = _dot(ckv.astype(BF16), wkv)
    n = ckv.shape[0]
    k_ref[row0:row0 + n, 0:NOPE_DIM] = kv[:, :NOPE_DIM].astype(BF16)
    k_ref[row0:row0 + n, NOPE_DIM:HEAD_PAD] = jnp.concatenate(
        [kpe, jnp.zeros((n, HEAD_PAD - QK_DIM), F32)], axis=-1).astype(BF16)
    v_ref[row0:row0 + n, :] = kv[:, NOPE_DIM:].astype(BF16)


def _attn_kernel(t, past, q_ref, ckv_ref, kpe_ref, wkv_ref, *rest):
    if past:
        cckv_ref, ckpe_ref, o_ref, k_ref, v_ref = rest
    else:
        o_ref, k_ref, v_ref = rest

    @pl.when(pl.program_id(2) == 0)
    def _():
        wkv = wkv_ref[...]
        if past:
            _fill_kv(k_ref, v_ref, 0, cckv_ref[0], ckpe_ref[0], wkv)
        step = min(t, KV_ROWS)
        for r0 in range(0, t, step):
            _fill_kv(k_ref, v_ref, past + r0, ckv_ref[r0:r0 + step, :], kpe_ref[r0:r0 + step, :], wkv)

    s = lax.dot_general(q_ref[...], k_ref[...], (((1,), (1,)), ((), ())),
                        preferred_element_type=F32) * ATTN_SCALE
    m = jnp.max(s, axis=-1, keepdims=True)
    e = jnp.exp(s - m)
    l = jnp.sum(e, axis=-1, keepdims=True)
    o_ref[...] = (_dot(e.astype(BF16), v_ref[...]) * (1.0 / l)).astype(BF16)


def _attn_call(q, ckv, kpe, wkv, batch, t, cache, tq):
    nq = t // tq
    past = 0 if cache is None else cache[0].shape[1]
    tk = past + t
    in_specs = [
        pl.BlockSpec((tq, HEAD_PAD), lambda b, h, i: (b * nq + i, h)),
        pl.BlockSpec((t, KV_RANK), lambda b, h, i: (b, 0)),
        pl.BlockSpec((t, ROPE_DIM), lambda b, h, i: (b, 0)),
        pl.BlockSpec((KV_RANK, NOPE_DIM + V_DIM), lambda b, h, i: (0, h)),
    ]
    args = [q, ckv, kpe, wkv]
    if past:
        in_specs += [
            pl.BlockSpec((1, past, KV_RANK), lambda b, h, i: (b, 0, 0)),
            pl.BlockSpec((1, past, ROPE_DIM), lambda b, h, i: (b, 0, 0)),
        ]
        args += list(cache)
    return pl.pallas_call(
        functools.partial(_attn_kernel, t, past),
        grid=(batch, MLA_HEADS, nq),
        in_specs=in_specs,
        out_specs=pl.BlockSpec((tq, V_DIM), lambda b, h, i: (b * nq + i, h)),
        out_shape=jax.ShapeDtypeStruct((batch * t, MLA_HEADS * V_DIM), BF16),
        scratch_shapes=[
            pltpu.VMEM((tk, HEAD_PAD), BF16),
            pltpu.VMEM((tk, V_DIM), BF16),
        ],
        compiler_params=_cparams(("arbitrary", "arbitrary", "arbitrary"), 48),
        name="attention",
    )(*args)


def _out_proj_kernel(n_in, x_ref, mods_ref, *rest):
    in_refs = rest[:n_in]
    w_ref, o_ref = rest[n_in:]
    acc = None
    k0 = 0
    for r in in_refs:
        k = r.shape[-1]
        part = _dot(r[...], w_ref[k0:k0 + k, :])
        acc = part if acc is None else acc + part
        k0 += k
    o_ref[...] = x_ref[...] + mods_ref[0][2:3, :] * acc


def _out_proj_call(x, mods, rows_per_cond, parts, w_o, tm=512):
    r = x.shape[0]
    row = lambda i: (i, 0)
    in_specs = [
        pl.BlockSpec((tm, D_MODEL), row),
        pl.BlockSpec((1, N_MODS, D_MODEL), lambda i: ((i * tm) // rows_per_cond, 0, 0)),
    ]
    in_specs += [pl.BlockSpec((tm, p.shape[-1]), row) for p in parts]
    in_specs += [pl.BlockSpec(w_o.shape, lambda i: (0, 0))]
    return pl.pallas_call(
        functools.partial(_out_proj_kernel, len(parts)),
        grid=(r // tm,),
        in_specs=in_specs,
        out_specs=pl.BlockSpec((tm, D_MODEL), row),
        out_shape=jax.ShapeDtypeStruct((r, D_MODEL), F32),
        compiler_params=_cparams(("arbitrary",), 48),
        name="out_proj",
    )(x, mods, *parts, w_o)


def _odd_in_kernel(tm, x_ref, mods_ref, g_ref, w_ref, lg_ref, lb_ref, ws_ref, bs_ref, o_ref):
    h = _modnorm(x_ref[...], g_ref[...], mods_ref[0], 0).astype(BF16)
    v = _gelu_tanh(_dot(h, w_ref[:, D_CMLP:]))
    v = _layernorm(v, lg_ref[...], lb_ref[...]).astype(BF16)
    u = _gelu_tanh(_dot(h, w_ref[:, :D_CMLP]))
    for ch in range(tm // CHUNK):
        rs = slice(ch * CHUNK, (ch + 1) * CHUNK)
        for g in range(CMLP_GROUPS):
            cs = slice(g * CMLP_GROUP_W, (g + 1) * CMLP_GROUP_W)
            mixed = _dot(ws_ref[g], v[rs, cs]) + bs_ref[:, cs]
            o_ref[rs, cs] = (u[rs, cs] * mixed).astype(BF16)


def _odd_in_call(x, mods, rows_per_cond, g_mix, w_in, ln_g, ln_b, w_s, b_s_full, tm=256):
    r = x.shape[0]
    const = lambda i: (0, 0)
    row = lambda i: (i, 0)
    return pl.pallas_call(
        functools.partial(_odd_in_kernel, tm),
        grid=(r // tm,),
        in_specs=[
            pl.BlockSpec((tm, D_MODEL), row),
            pl.BlockSpec((1, N_MODS, D_MODEL), lambda i: ((i * tm) // rows_per_cond, 0, 0)),
            pl.BlockSpec((1, D_MODEL), const),
            pl.BlockSpec(w_in.shape, const, pipeline_mode=pl.Buffered(1)),
            pl.BlockSpec((1, D_CMLP), const),
            pl.BlockSpec((1, D_CMLP), const),
            pl.BlockSpec(w_s.shape, lambda i: (0, 0, 0)),
            pl.BlockSpec((CHUNK, D_CMLP), const),
        ],
        out_specs=pl.BlockSpec((tm, D_CMLP), row),
        out_shape=jax.ShapeDtypeStruct((r, D_CMLP), BF16),
        compiler_params=_cparams(("arbitrary",), 56),
        name="odd_in",
    )(x, mods, g_mix, w_in, ln_g, ln_b, w_s, b_s_full)


def _ffn_kernel(final, x_ref, mods_ref, g_ref, wg_ref, wu_ref, wd_ref, *rest):
    if final:
        fg_ref, o_ref, h_ref, acc_ref = rest
    else:
        o_ref, h_ref, acc_ref = rest
    j = pl.program_id(1)

    @pl.when(j == 0)
    def _():
        h_ref[...] = _modnorm(x_ref[...], g_ref[...], mods_ref[0], 3).astype(BF16)
        acc_ref[...] = jnp.zeros_like(acc_ref)

    h = h_ref[...]
    hid = (_silu(_dot(h, wg_ref[...])) * _dot(h, wu_ref[...])).astype(BF16)
    acc_ref[...] += _dot(hid, wd_ref[...])

    @pl.when(j == pl.num_programs(1) - 1)
    def _():
        y = x_ref[...] + mods_ref[0][5:6, :] * acc_ref[...]
        if final:
            y = _rms(y, fg_ref[...])
        o_ref[...] = y


def _ffn_call(x, mods, rows_per_cond, g_ffn, wg, wu, wd, final_g=None, tm=512, tn=512):
    r = x.shape[0]
    final = final_g is not None
    const = lambda i, j: (0, 0)
    row = lambda i, j: (i, 0)
    in_specs = [
        pl.BlockSpec((tm, D_MODEL), row),
        pl.BlockSpec((1, N_MODS, D_MODEL), lambda i, j: ((i * tm) // rows_per_cond, 0, 0)),
        pl.BlockSpec((1, D_MODEL), const),
        pl.BlockSpec((D_MODEL, tn), lambda i, j: (0, j)),
        pl.BlockSpec((D_MODEL, tn), lambda i, j: (0, j)),
        pl.BlockSpec((tn, D_MODEL), lambda i, j: (j, 0)),
    ]
    args = [x, mods, g_ffn, wg, wu, wd]
    if final:
        in_specs.append(pl.BlockSpec((1, D_MODEL), const))
        args.append(final_g)
    return pl.pallas_call(
        functools.partial(_ffn_kernel, final),
        grid=(r // tm, D_FF // tn),
        in_specs=in_specs,
        out_specs=pl.BlockSpec((tm, D_MODEL), row),
        out_shape=jax.ShapeDtypeStruct((r, D_MODEL), F32),
        scratch_shapes=[
            pltpu.VMEM((tm, D_MODEL), BF16),
            pltpu.VMEM((tm, D_MODEL), F32),
        ],
        compiler_params=_cparams(("arbitrary", "arbitrary"), 56),
        name="ffn",
    )(*args)


def _rope_tables(n_tokens):
    n_freq = ROPE_DIM // 4
    lane = jnp.arange(ROPE_DIM)
    tok = jnp.arange(n_tokens)
    pos = jnp.where(lane[None, :] // (2 * n_freq) == 0, tok[:, None] // GRID_W, tok[:, None] % GRID_W).astype(F32)
    inv_freq = jnp.power(ROPE_THETA, -jnp.arange(n_freq, dtype=F32) * 2.0 / (ROPE_DIM // 2))
    ang = pos * inv_freq[lane % n_freq][None, :]
    sign = jnp.where((lane // n_freq) % 2 == 0, -1.0, 1.0).astype(F32)
    return jnp.cos(ang), jnp.sin(ang) * sign[None, :]


def _pair_swap(w):
    perm = jnp.arange(ROPE_DIM) ^ (ROPE_DIM // 4)
    return w[..., perm]


def kernel(x_prompt, x_sample, cache_ckv, cache_kpe, c, c_ctx, mod_w, mod_b, norm_mix_g, norm_ffn_g, ffn_w_gate, ffn_w_up, ffn_w_down, ev_w_in, ev_conv_w, ev_conv_b, ev_conv_ln_g, ev_conv_ln_b, ev_q_norm_g, ev_w_qb, ev_kv_norm_g, ev_w_kvb, ev_w_o, od_w_in, od_ln_g, od_ln_b, od_w_s, od_b_s, od_w_o, final_norm_g):
    bp, tp, _ = x_prompt.shape
    bs, ts, _ = x_sample.shape
    xp = x_prompt.reshape(bp * tp, D_MODEL)
    xs = x_sample.reshape(bs * ts, D_MODEL)

    cond8 = jnp.concatenate([c_ctx[None, :], c, jnp.zeros((N_MODS - 1 - bs, D_MODEL), F32)], axis=0)
    mods = _mods_call(cond8, mod_w, mod_b)
    mods = mods[:, :1 + bs].reshape(DEPTH, 1 + bs, 6, D_MODEL)
    mods = jnp.pad(mods, ((0, 0), (0, 0), (0, N_MODS - 6), (0, 0)))

    cos_k, sin_k = _rope_tables(ts)
    zpad = jnp.zeros((ts, LANES - ROPE_DIM), F32)
    cos_q = jnp.concatenate([cos_k, zpad], axis=-1)
    sin_q = jnp.concatenate([sin_k, zpad], axis=-1)

    streams = [
        dict(x=xp, batch=bp, t=tp, rows=bp * tp, conds=slice(0, 1), rope=False),
        dict(x=xs, batch=bs, t=ts, rows=ts, conds=slice(1, 1 + bs), rope=True),
    ]
    state = {}
    for l in range(DEPTH):
        g_mix = norm_mix_g[l][None, :]
        g_ffn = norm_ffn_g[l][None, :]
        wg = ffn_w_gate[l].astype(BF16)
        wu = ffn_w_up[l].astype(BF16)
        wd = ffn_w_down[l].astype(BF16)
        last = l == DEPTH - 1
        if l % 2 == 0:
            e = l // 2
            w_in = ev_w_in[e]
            w1 = jnp.concatenate([w_in, _pair_swap(w_in[:, O_KV:])], axis=1).astype(BF16)
            wq3 = ev_w_qb[e].reshape(Q_RANK, MLA_HEADS, QK_DIM)
            wq = jnp.pad(wq3, ((0, 0), (0, 0), (0, HEAD_PAD - QK_DIM))).reshape(Q_RANK, -1).astype(BF16)
            wqs = jnp.pad(_pair_swap(wq3[:, :, NOPE_DIM:]), ((0, 0), (0, 0), (0, LANES - ROPE_DIM)))
            wqs = wqs.reshape(Q_RANK, -1).astype(BF16)
            wkv = ev_w_kvb[e].astype(BF16)
            w_o = ev_w_o[e].astype(BF16)
            conv_w = ev_conv_w[e].reshape(CONV_WIDTH, D_CONV)
            for s in streams:
                m = mods[l, s["conds"]]
                rope_args = (wqs, cos_q, sin_q, cos_k, sin_k) if s["rope"] else None
                a, q, ckv, kpe = _even_in_call(s["x"], m, s["rows"], g_mix, w1, ev_q_norm_g[e][None, :],
                                               ev_kv_norm_g[e][None, :], wq, rope_args)
                a = _conv_call(a, s["batch"], s["t"], conv_w, ev_conv_b[e][None, :],
                               ev_conv_ln_g[e][None, :], ev_conv_ln_b[e][None, :], tt=min(s["t"], 256))
                cache = (cache_ckv[:, e], cache_kpe[:, e]) if s["rope"] else None
                att = _attn_call(q, ckv, kpe, wkv, s["batch"], s["t"], cache, tq=256)
                if not s["rope"]:
                    state.setdefault("ckv", []).append(ckv.reshape(bp, tp, KV_RANK))
                    state.setdefault("kpe", []).append(kpe.reshape(bp, tp, ROPE_DIM))
                s["x"] = _out_proj_call(s["x"], m, s["rows"], [a, att], w_o)
        else:
            o = l // 2
            w_in = od_w_in[o].astype(BF16)
            w_o = od_w_o[o].astype(BF16)
            w_s = od_w_s[o].astype(BF16)
            b_s_full = jnp.repeat(od_b_s[o].T, CMLP_GROUP_W, axis=1)
            for s in streams:
                m = mods[l, s["conds"]]
                p = _odd_in_call(s["x"], m, s["rows"], g_mix, w_in, od_ln_g[o][None, :], od_ln_b[o][None, :],
                                 w_s, b_s_full)
                s["x"] = _out_proj_call(s["x"], m, s["rows"], [p], w_o)
        for s in streams:
            m = mods[l, s["conds"]]
            s["x"] = _ffn_call(s["x"], m, s["rows"], g_ffn, wg, wu, wd,
                               final_g=final_norm_g[None, :] if last else None)

    y_prompt = streams[0]["x"].reshape(bp, tp, D_MODEL)
    y_sample = streams[1]["x"].reshape(bs, ts, D_MODEL)
    state_ckv = jnp.stack(state["ckv"], axis=1)
    state_kpe = jnp.stack(state["kpe"], axis=1)
    return (y_prompt, y_sample, state_ckv, state_kpe)
```

```python
import functools

import jax
import jax.numpy as jnp
from jax import lax
from jax.experimental import pallas as pl
from jax.experimental.pallas import tpu as pltpu

D_MODEL = 2048
DEPTH = 2
GRID_W = 64
D_CONV = D_MODEL // 2
CONV_WIDTH = 31
CONV_PAD = CONV_WIDTH // 2
MLA_HEADS = D_MODEL // 256
Q_RANK = D_MODEL // 4
KV_RANK = D_MODEL // 8
NOPE_DIM = 128
ROPE_DIM = 64
V_DIM = 128
QK_DIM = NOPE_DIM + ROPE_DIM
ROPE_THETA = 10000.0
D_CMLP = D_MODEL
CHUNK = 128
CMLP_GROUPS = 8
CMLP_GROUP_W = D_CMLP // CMLP_GROUPS
D_FF = 256 * (-(-8 * D_MODEL // (3 * 256)))
ATTN_SCALE = QK_DIM ** -0.5
EPS = 1e-6

LANES = 128
SUBLANES = 8
HEAD_PAD = 2 * LANES
CONV_HALO = 2 * SUBLANES
N_MODS = 8
MIB = 1024 * 1024

BF16 = jnp.bfloat16
F32 = jnp.float32


def _cparams(semantics, vmem_mib):
    return pltpu.CompilerParams(dimension_semantics=semantics, vmem_limit_bytes=vmem_mib * MIB)


def _silu(x):
    return x * (1.0 / (1.0 + jnp.exp(-x)))


def _gelu_tanh(x):
    cdf = 0.5 * (1.0 + jnp.tanh(0.7978845608028654 * (x + 0.044715 * (x * x * x))))
    return x * cdf


def _rms(x, g):
    return (x * lax.rsqrt(jnp.mean(x * x, axis=-1, keepdims=True) + EPS)) * g


def _layernorm(x, g, b):
    mu = jnp.mean(x, axis=-1, keepdims=True)
    xc = x - mu
    var = jnp.mean(xc * xc, axis=-1, keepdims=True)
    return xc * lax.rsqrt(var + EPS) * g + b


def _modnorm(x, g, mods, row):
    return _rms(x, g) * (1.0 + mods[row + 1:row + 2, :]) + mods[row:row + 1, :]


def _dot(a, b):
    return jnp.dot(a, b, preferred_element_type=F32)


def _mods_kernel(cond_ref, w_ref, b_ref, o_ref):
    s = _silu(cond_ref[...]).astype(BF16)
    o_ref[0] = _dot(s, w_ref[0].astype(BF16)) + b_ref[0]


def _mods_call(cond8, mod_w, mod_b):
    tn = 1024
    n = mod_w.shape[-1]
    return pl.pallas_call(
        _mods_kernel,
        grid=(DEPTH, n // tn),
        in_specs=[
            pl.BlockSpec((N_MODS, D_MODEL), lambda l, j: (0, 0)),
            pl.BlockSpec((1, D_MODEL, tn), lambda l, j: (l, 0, j)),
            pl.BlockSpec((1, 1, tn), lambda l, j: (l, 0, j)),
        ],
        out_specs=pl.BlockSpec((1, N_MODS, tn), lambda l, j: (l, 0, j)),
        out_shape=jax.ShapeDtypeStruct((DEPTH, N_MODS, n), F32),
        compiler_params=_cparams(("arbitrary", "arbitrary"), 32),
        name="mods",
    )(cond8, mod_w, mod_b.reshape(DEPTH, 1, n))


O_GLU = 2 * D_CONV
O_Q = O_GLU + Q_RANK
O_KV = O_Q + KV_RANK
O_END = O_KV + 2 * ROPE_DIM


def _even_in_kernel(rope, x_ref, mods_ref, g_ref, w_ref, gq_ref, gkv_ref, wq_ref, *rest):
    if rope:
        wqs_ref, cq_ref, sq_ref, ck_ref, sk_ref, a_ref, q_ref, ckv_ref, kpe_ref = rest
    else:
        a_ref, q_ref, ckv_ref, kpe_ref = rest
    h = _modnorm(x_ref[...], g_ref[...], mods_ref[0], 0).astype(BF16)
    lin = _dot(h, w_ref[:, :D_CONV])
    gate = _dot(h, w_ref[:, D_CONV:O_GLU])
    a_ref[...] = lin * (1.0 / (1.0 + jnp.exp(-gate)))
    qn = _rms(_dot(h, w_ref[:, O_GLU:O_Q]), gq_ref[...]).astype(BF16)
    ckv_ref[...] = _rms(_dot(h, w_ref[:, O_Q:O_KV]), gkv_ref[...])
    kp = _dot(h, w_ref[:, O_KV:O_END])
    q = _dot(qn, wq_ref[...])
    if rope:
        kpe_ref[...] = kp[:, :ROPE_DIM] * ck_ref[...] + kp[:, ROPE_DIM:] * sk_ref[...]
        qs = _dot(qn, wqs_ref[...])
        cq = cq_ref[...]
        sq = sq_ref[...]
        for hd in range(MLA_HEADS):
            c0 = hd * HEAD_PAD
            q_ref[:, c0:c0 + LANES] = q[:, c0:c0 + LANES].astype(BF16)
            q_ref[:, c0 + LANES:c0 + HEAD_PAD] = (
                q[:, c0 + LANES:c0 + HEAD_PAD] * cq + qs[:, hd * LANES:(hd + 1) * LANES] * sq).astype(BF16)
    else:
        kpe_ref[...] = kp[:, :ROPE_DIM]
        q_ref[...] = q.astype(BF16)


def _even_in_call(x, mods, rows_per_cond, g_mix, w1, g_qa, g_kva, wq, rope_args, tm=256):
    r = x.shape[0]
    rope = rope_args is not None
    const = lambda i: (0, 0)
    row = lambda i: (i, 0)
    in_specs = [
        pl.BlockSpec((tm, D_MODEL), row),
        pl.BlockSpec((1, N_MODS, D_MODEL), lambda i: ((i * tm) // rows_per_cond, 0, 0)),
        pl.BlockSpec((1, D_MODEL), const),
        pl.BlockSpec(w1.shape, const),
        pl.BlockSpec((1, Q_RANK), const),
        pl.BlockSpec((1, KV_RANK), const),
        pl.BlockSpec(wq.shape, const),
    ]
    args = [x, mods, g_mix, w1, g_qa, g_kva, wq]
    if rope:
        wqs, cq, sq, ck, sk = rope_args
        t_blocks = cq.shape[0] // tm
        pos = lambda i: (i % t_blocks, 0)
        in_specs += [
            pl.BlockSpec(wqs.shape, const),
            pl.BlockSpec((tm, LANES), pos),
            pl.BlockSpec((tm, LANES), pos),
            pl.BlockSpec((tm, ROPE_DIM), pos),
            pl.BlockSpec((tm, ROPE_DIM), pos),
        ]
        args += [wqs, cq, sq, ck, sk]
    return pl.pallas_call(
        functools.partial(_even_in_kernel, rope),
        grid=(r // tm,),
        in_specs=in_specs,
        out_specs=[
            pl.BlockSpec((tm, D_CONV), row),
            pl.BlockSpec((tm, MLA_HEADS * HEAD_PAD), row),
            pl.BlockSpec((tm, KV_RANK), row),
            pl.BlockSpec((tm, ROPE_DIM), row),
        ],
        out_shape=[
            jax.ShapeDtypeStruct((r, D_CONV), F32),
            jax.ShapeDtypeStruct((r, MLA_HEADS * HEAD_PAD), BF16),
            jax.ShapeDtypeStruct((r, KV_RANK), F32),
            jax.ShapeDtypeStruct((r, ROPE_DIM), F32),
        ],
        compiler_params=_cparams(("arbitrary",), 56),
        name="even_in",
    )(*args)


CONV_ROWS = 32


def _conv_kernel(tt, n_t, cur_ref, prev_ref, next_ref, w_ref, b_ref, g_ref, beta_ref, o_ref, buf_ref, sh_ref, y_ref):
    i = pl.program_id(1)
    keep_prev = (i > 0).astype(F32)
    keep_next = (i < n_t - 1).astype(F32)
    buf_ref[0:CONV_HALO, :] = prev_ref[0] * keep_prev
    buf_ref[CONV_HALO:CONV_HALO + tt, :] = cur_ref[0]
    buf_ref[CONV_HALO + tt:, :] = next_ref[0] * keep_next
    n_sh = tt + 2 * CONV_HALO - SUBLANES
    for r in range(1, SUBLANES):
        sh_ref[r - 1] = buf_ref[r:r + n_sh, :]
    first = CONV_HALO - CONV_PAD
    n_win = CONV_ROWS + (first + CONV_WIDTH - 1) // SUBLANES * SUBLANES
    for c in range(D_CONV // LANES):
        cs = slice(c * LANES, (c + 1) * LANES)
        for r0 in range(0, tt, CONV_ROWS):
            acc = b_ref[:, cs]
            for r in range(SUBLANES):
                win = buf_ref[r0:r0 + n_win, cs] if r == 0 else sh_ref[r - 1, r0:r0 + n_win, cs]
                for q in range(n_win // SUBLANES):
                    k = q * SUBLANES + r - first
                    if 0 <= k < CONV_WIDTH:
                        acc = acc + win[q * SUBLANES:q * SUBLANES + CONV_ROWS, :] * w_ref[k:k + 1, cs]
            y_ref[r0:r0 + CONV_ROWS, cs] = acc
    o_ref[0] = _silu(_layernorm(y_ref[...], g_ref[...], beta_ref[...])).astype(BF16)


def _conv_call(a, batch, t, conv_w, conv_b, ln_g, ln_b, tt):
    n_t = t // tt
    a3 = a.reshape(batch, t, D_CONV)
    hb = tt // CONV_HALO
    n_hb = t // CONV_HALO
    const = lambda b, i: (0, 0)
    out = pl.pallas_call(
        functools.partial(_conv_kernel, tt, n_t),
        grid=(batch, n_t),
        in_specs=[
            pl.BlockSpec((1, tt, D_CONV), lambda b, i: (b, i, 0)),
            pl.BlockSpec((1, CONV_HALO, D_CONV), lambda b, i: (b, jnp.maximum(i * hb - 1, 0), 0)),
            pl.BlockSpec((1, CONV_HALO, D_CONV), lambda b, i: (b, jnp.minimum((i + 1) * hb, n_hb - 1), 0)),
            pl.BlockSpec((CONV_WIDTH, D_CONV), const),
            pl.BlockSpec((1, D_CONV), const),
            pl.BlockSpec((1, D_CONV), const),
            pl.BlockSpec((1, D_CONV), const),
        ],
        out_specs=pl.BlockSpec((1, tt, D_CONV), lambda b, i: (b, i, 0)),
        out_shape=jax.ShapeDtypeStruct((batch, t, D_CONV), BF16),
        scratch_shapes=[
            pltpu.VMEM((tt + 2 * CONV_HALO, D_CONV), F32),
            pltpu.VMEM((SUBLANES - 1, tt + 2 * CONV_HALO - SUBLANES, D_CONV), F32),
            pltpu.VMEM((tt, D_CONV), F32),
        ],
        compiler_params=_cparams(("arbitrary", "arbitrary"), 32),
        name="conv",
    )(a3, a3, a3, conv_w, conv_b, ln_g, ln_b)
    return out.reshape(batch * t, D_CONV)


KV_ROWS = 1024


KEY_CHUNK = 512
EXP2_SCALE = ATTN_SCALE * 1.4426950408889634
NT_DIMS = (((1,), (1,)), ((), ()))


def _rope_key_pad(kpe):
    return jnp.concatenate([kpe, jnp.zeros((kpe.shape[0], HEAD_PAD - QK_DIM), F32)], axis=-1).astype(BF16)


def _fill_kv(k_ref, vt_ref, row0, ckv, kpe, wk, wvt):
    c = ckv.astype(BF16)
    n = c.shape[0]
    k_ref[row0:row0 + n, 0:NOPE_DIM] = _dot(c, wk).astype(BF16)
    k_ref[row0:row0 + n, NOPE_DIM:HEAD_PAD] = _rope_key_pad(kpe)
    vt_ref[:, row0:row0 + n] = lax.dot_general(wvt, c, NT_DIMS, preferred_element_type=F32).astype(BF16)


def _attn_kernel(t, past, chunks, q_ref, ckv_ref, kpe_ref, wk_ref, wvt_ref, cckv_ref, ckpe_ref, o_ref, k_ref, vt_ref):
    @pl.when(pl.program_id(2) == 0)
    def _():
        wk = wk_ref[...]
        wvt = wvt_ref[...]
        _fill_kv(k_ref, vt_ref, 0, cckv_ref[0], ckpe_ref[0], wk, wvt)
        for r0 in range(0, t, KV_ROWS):
            _fill_kv(k_ref, vt_ref, past + r0, ckv_ref[r0:r0 + KV_ROWS, :], kpe_ref[r0:r0 + KV_ROWS, :], wk, wvt)

    q = q_ref[...]

    def scores(chunk):
        k0, kn = chunk
        return lax.dot_general(k_ref[k0:k0 + kn, :], q, NT_DIMS, preferred_element_type=F32)

    m = l = acc = None
    s_next = scores(chunks[0])
    for ci, (k0, kn) in enumerate(chunks):
        s = s_next
        if ci + 1 < len(chunks):
            s_next = scores(chunks[ci + 1])
        mc = jnp.max(s, axis=0, keepdims=True)
        m_new = mc if m is None else jnp.maximum(m, mc)
        e = jnp.exp2((s - m_new) * EXP2_SCALE)
        ls = jnp.sum(e, axis=0, keepdims=True)
        pv = _dot(vt_ref[:, k0:k0 + kn], e.astype(BF16))
        if m is None:
            l, acc = ls, pv
        else:
            alpha = jnp.exp2((m - m_new) * EXP2_SCALE)
            l = alpha * l + ls
            acc = alpha * acc + pv
        m = m_new
    o_ref[...] = (acc * (1.0 / l)).T.astype(BF16)


def _attn_call(q, ckv, kpe, wkv, wkv_t, batch, t, cache, tq):
    nq = t // tq
    past = cache[0].shape[1]
    tk = past + t
    assert t % KV_ROWS == 0 and past % LANES == 0
    chunks = [(0, past + KEY_CHUNK)] + [(k0, KEY_CHUNK) for k0 in range(past + KEY_CHUNK, tk, KEY_CHUNK)]
    return pl.pallas_call(
        functools.partial(_attn_kernel, t, past, chunks),
        grid=(batch, MLA_HEADS, nq),
        in_specs=[
            pl.BlockSpec((tq, HEAD_PAD), lambda b, h, i: (b * nq + i, h)),
            pl.BlockSpec((t, KV_RANK), lambda b, h, i: (b, 0)),
            pl.BlockSpec((t, ROPE_DIM), lambda b, h, i: (b, 0)),
            pl.BlockSpec((KV_RANK, NOPE_DIM), lambda b, h, i: (0, 2 * h)),
            pl.BlockSpec((V_DIM, KV_RANK), lambda b, h, i: (2 * h + 1, 0)),
            pl.BlockSpec((1, past, KV_RANK), lambda b, h, i: (b, 0, 0)),
            pl.BlockSpec((1, past, ROPE_DIM), lambda b, h, i: (b, 0, 0)),
        ],
        out_specs=pl.BlockSpec((tq, V_DIM), lambda b, h, i: (b * nq + i, h)),
        out_shape=jax.ShapeDtypeStruct((batch * t, MLA_HEADS * V_DIM), BF16),
        scratch_shapes=[
            pltpu.VMEM((tk, HEAD_PAD), BF16),
            pltpu.VMEM((V_DIM, tk), BF16),
        ],
        compiler_params=_cparams(("arbitrary", "arbitrary", "arbitrary"), 48),
        name="attention",
    )(q, ckv, kpe, wkv, wkv_t, *cache)


def _attn_seq_kernel(q_ref, ckv_ref, kpe_ref, wkv_ref, o_ref):
    kv = _dot(ckv_ref[...].astype(BF16), wkv_ref[...])
    kpad = _rope_key_pad(kpe_ref[...])
    for hd in range(MLA_HEADS):
        c0 = hd * (NOPE_DIM + V_DIM)
        k = jnp.concatenate([kv[:, c0:c0 + NOPE_DIM].astype(BF16), kpad], axis=-1)
        s = lax.dot_general(q_ref[:, hd * HEAD_PAD:(hd + 1) * HEAD_PAD], k, NT_DIMS, preferred_element_type=F32)
        m = jnp.max(s, axis=-1, keepdims=True)
        e = jnp.exp2((s - m) * EXP2_SCALE)
        l = jnp.sum(e, axis=-1, keepdims=True)
        o = _dot(e.astype(BF16), kv[:, c0 + NOPE_DIM:c0 + NOPE_DIM + V_DIM].astype(BF16)) * (1.0 / l)
        o_ref[:, hd * V_DIM:(hd + 1) * V_DIM] = o.astype(BF16)


def _attn_seq_call(q, ckv, kpe, wkv, batch, t):
    row = lambda b: (b, 0)
    return pl.pallas_call(
        _attn_seq_kernel,
        grid=(batch,),
        in_specs=[
            pl.BlockSpec((t, MLA_HEADS * HEAD_PAD), row),
            pl.BlockSpec((t, KV_RANK), row),
            pl.BlockSpec((t, ROPE_DIM), row),
            pl.BlockSpec(wkv.shape, lambda b: (0, 0)),
        ],
        out_specs=pl.BlockSpec((t, MLA_HEADS * V_DIM), row),
        out_shape=jax.ShapeDtypeStruct((batch * t, MLA_HEADS * V_DIM), BF16),
        compiler_params=_cparams(("arbitrary",), 32),
        name="attention_seq",
    )(q, ckv, kpe, wkv)


def _out_proj_kernel(n_in, x_ref, mods_ref, *rest):
    in_refs = rest[:n_in]
    w_ref, o_ref = rest[n_in:]
    acc = None
    k0 = 0
    for r in in_refs:
        k = r.shape[-1]
        part = _dot(r[...], w_ref[k0:k0 + k, :])
        acc = part if acc is None else acc + part
        k0 += k
    o_ref[...] = x_ref[...] + mods_ref[0][2:3, :] * acc


def _out_proj_call(x, mods, rows_per_cond, parts, w_o, tm=512):
    r = x.shape[0]
    row = lambda i: (i, 0)
    in_specs = [
        pl.BlockSpec((tm, D_MODEL), row),
        pl.BlockSpec((1, N_MODS, D_MODEL), lambda i: ((i * tm) // rows_per_cond, 0, 0)),
    ]
    in_specs += [pl.BlockSpec((tm, p.shape[-1]), row) for p in parts]
    in_specs += [pl.BlockSpec(w_o.shape, lambda i: (0, 0))]
    return pl.pallas_call(
        functools.partial(_out_proj_kernel, len(parts)),
        grid=(r // tm,),
        in_specs=in_specs,
        out_specs=pl.BlockSpec((tm, D_MODEL), row),
        out_shape=jax.ShapeDtypeStruct((r, D_MODEL), F32),
        compiler_params=_cparams(("arbitrary",), 48),
        name="out_proj",
    )(x, mods, *parts, w_o)


def _odd_in_kernel(tm, x_ref, mods_ref, g_ref, w_ref, lg_ref, lb_ref, ws_ref, bs_ref, o_ref):
    h = _modnorm(x_ref[...], g_ref[...], mods_ref[0], 0).astype(BF16)
    v = _gelu_tanh(_dot(h, w_ref[:, D_CMLP:]))
    v = _layernorm(v, lg_ref[...], lb_ref[...]).astype(BF16)
    u = _gelu_tanh(_dot(h, w_ref[:, :D_CMLP]))
    for ch in range(tm // CHUNK):
        rs = slice(ch * CHUNK, (ch + 1) * CHUNK)
        for g in range(CMLP_GROUPS):
            cs = slice(g * CMLP_GROUP_W, (g + 1) * CMLP_GROUP_W)
            mixed = _dot(ws_ref[g], v[rs, cs]) + bs_ref[:, cs]
            o_ref[rs, cs] = (u[rs, cs] * mixed).astype(BF16)


def _odd_in_call(x, mods, rows_per_cond, g_mix, w_in, ln_g, ln_b, w_s, b_s_full, tm=256):
    r = x.shape[0]
    const = lambda i: (0, 0)
    row = lambda i: (i, 0)
    return pl.pallas_call(
        functools.partial(_odd_in_kernel, tm),
        grid=(r // tm,),
        in_specs=[
            pl.BlockSpec((tm, D_MODEL), row),
            pl.BlockSpec((1, N_MODS, D_MODEL), lambda i: ((i * tm) // rows_per_cond, 0, 0)),
            pl.BlockSpec((1, D_MODEL), const),
            pl.BlockSpec(w_in.shape, const, pipeline_mode=pl.Buffered(1)),
            pl.BlockSpec((1, D_CMLP), const),
            pl.BlockSpec((1, D_CMLP), const),
            pl.BlockSpec(w_s.shape, lambda i: (0, 0, 0)),
            pl.BlockSpec((CHUNK, D_CMLP), const),
        ],
        out_specs=pl.BlockSpec((tm, D_CMLP), row),
        out_shape=jax.ShapeDtypeStruct((r, D_CMLP), BF16),
        compiler_params=_cparams(("arbitrary",), 56),
        name="odd_in",
    )(x, mods, g_mix, w_in, ln_g, ln_b, w_s, b_s_full)


def _ffn_kernel(final, x_ref, mods_ref, g_ref, wg_ref, wu_ref, wd_ref, *rest):
    if final:
        fg_ref, o_ref, h_ref = rest
    else:
        o_ref, h_ref = rest
    j = pl.program_id(1)

    @pl.when(j == 0)
    def _():
        h_ref[...] = _modnorm(x_ref[...], g_ref[...], mods_ref[0], 3).astype(BF16)
        o_ref[...] = jnp.zeros_like(o_ref)

    h = h_ref[...]
    hid = (_silu(_dot(h, wg_ref[...].astype(BF16))) * _dot(h, wu_ref[...].astype(BF16))).astype(BF16)
    o_ref[...] += _dot(hid, wd_ref[...].astype(BF16))

    @pl.when(j == pl.num_programs(1) - 1)
    def _():
        y = x_ref[...] + mods_ref[0][5:6, :] * o_ref[...]
        if final:
            y = _rms(y, fg_ref[...])
        o_ref[...] = y


def _ffn_call(x, mods, rows_per_cond, g_ffn, layer, wg, wu, wd, final_g=None, tm=1024, tn=256):
    r = x.shape[0]
    final = final_g is not None
    const = lambda i, j: (0, 0)
    row = lambda i, j: (i, 0)
    in_specs = [
        pl.BlockSpec((tm, D_MODEL), row),
        pl.BlockSpec((1, N_MODS, D_MODEL), lambda i, j: ((i * tm) // rows_per_cond, 0, 0)),
        pl.BlockSpec((1, D_MODEL), const),
        pl.BlockSpec((None, D_MODEL, tn), lambda i, j: (layer, 0, j)),
        pl.BlockSpec((None, D_MODEL, tn), lambda i, j: (layer, 0, j)),
        pl.BlockSpec((None, tn, D_MODEL), lambda i, j: (layer, j, 0)),
    ]
    args = [x, mods, g_ffn, wg, wu, wd]
    if final:
        in_specs.append(pl.BlockSpec((1, D_MODEL), const))
        args.append(final_g)
    return pl.pallas_call(
        functools.partial(_ffn_kernel, final),
        grid=(r // tm, D_FF // tn),
        in_specs=in_specs,
        out_specs=pl.BlockSpec((tm, D_MODEL), row),
        out_shape=jax.ShapeDtypeStruct((r, D_MODEL), F32),
        scratch_shapes=[pltpu.VMEM((tm, D_MODEL), BF16)],
        compiler_params=_cparams(("arbitrary", "arbitrary"), 60),
        name="ffn",
    )(*args)


def _rope_tables(n_tokens):
    n_freq = ROPE_DIM // 4
    lane = jnp.arange(ROPE_DIM)
    tok = jnp.arange(n_tokens)
    pos = jnp.where(lane[None, :] // (2 * n_freq) == 0, tok[:, None] // GRID_W, tok[:, None] % GRID_W).astype(F32)
    inv_freq = jnp.power(ROPE_THETA, -jnp.arange(n_freq, dtype=F32) * 2.0 / (ROPE_DIM // 2))
    ang = pos * inv_freq[lane % n_freq][None, :]
    sign = jnp.where((lane // n_freq) % 2 == 0, -1.0, 1.0).astype(F32)
    return jnp.cos(ang), jnp.sin(ang) * sign[None, :]


def _pair_swap(w):
    perm = jnp.arange(ROPE_DIM) ^ (ROPE_DIM // 4)
    return w[..., perm]


def kernel(x_prompt, x_sample, cache_ckv, cache_kpe, c, c_ctx, mod_w, mod_b, norm_mix_g, norm_ffn_g, ffn_w_gate, ffn_w_up, ffn_w_down, ev_w_in, ev_conv_w, ev_conv_b, ev_conv_ln_g, ev_conv_ln_b, ev_q_norm_g, ev_w_qb, ev_kv_norm_g, ev_w_kvb, ev_w_o, od_w_in, od_ln_g, od_ln_b, od_w_s, od_b_s, od_w_o, final_norm_g):
    bp, tp, _ = x_prompt.shape
    bs, ts, _ = x_sample.shape
    xp = x_prompt.reshape(bp * tp, D_MODEL)
    xs = x_sample.reshape(bs * ts, D_MODEL)

    cond8 = jnp.concatenate([c_ctx[None, :], c, jnp.zeros((N_MODS - 1 - bs, D_MODEL), F32)], axis=0)
    mods = _mods_call(cond8, mod_w, mod_b)
    mods = mods[:, :1 + bs].reshape(DEPTH, 1 + bs, 6, D_MODEL)
    mods = jnp.pad(mods, ((0, 0), (0, 0), (0, N_MODS - 6), (0, 0)))

    cos_k, sin_k = _rope_tables(ts)
    zpad = jnp.zeros((ts, LANES - ROPE_DIM), F32)
    cos_q = jnp.concatenate([cos_k, zpad], axis=-1)
    sin_q = jnp.concatenate([sin_k, zpad], axis=-1)

    streams = [
        dict(x=xp, batch=bp, t=tp, rows=bp * tp, conds=slice(0, 1), rope=False),
        dict(x=xs, batch=bs, t=ts, rows=ts, conds=slice(1, 1 + bs), rope=True),
    ]
    state = {}
    for l in range(DEPTH):
        g_mix = norm_mix_g[l][None, :]
        g_ffn = norm_ffn_g[l][None, :]
        last = l == DEPTH - 1
        if l % 2 == 0:
            e = l // 2
            w_in = ev_w_in[e]
            w1 = jnp.concatenate([w_in, _pair_swap(w_in[:, O_KV:])], axis=1).astype(BF16)
            wq3 = ev_w_qb[e].reshape(Q_RANK, MLA_HEADS, QK_DIM)
            wq = jnp.pad(wq3, ((0, 0), (0, 0), (0, HEAD_PAD - QK_DIM))).reshape(Q_RANK, -1).astype(BF16)
            wqs = jnp.pad(_pair_swap(wq3[:, :, NOPE_DIM:]), ((0, 0), (0, 0), (0, LANES - ROPE_DIM)))
            wqs = wqs.reshape(Q_RANK, -1).astype(BF16)
            wkv = ev_w_kvb[e].astype(BF16)
            wkv_t = ev_w_kvb[e].T.astype(BF16)
            w_o = ev_w_o[e].astype(BF16)
            conv_w = ev_conv_w[e].reshape(CONV_WIDTH, D_CONV)
            for s in streams:
                m = mods[l, s["conds"]]
                rope_args = (wqs, cos_q, sin_q, cos_k, sin_k) if s["rope"] else None
                a, q, ckv, kpe = _even_in_call(s["x"], m, s["rows"], g_mix, w1, ev_q_norm_g[e][None, :],
                                               ev_kv_norm_g[e][None, :], wq, rope_args)
                a = _conv_call(a, s["batch"], s["t"], conv_w, ev_conv_b[e][None, :],
                               ev_conv_ln_g[e][None, :], ev_conv_ln_b[e][None, :], tt=min(s["t"], 256))
                if s["rope"]:
                    att = _attn_call(q, ckv, kpe, wkv, wkv_t, s["batch"], s["t"],
                                     (cache_ckv[:, e], cache_kpe[:, e]), tq=512)
                else:
                    att = _attn_seq_call(q, ckv, kpe, wkv, s["batch"], s["t"])
                    state.setdefault("ckv", []).append(ckv.reshape(bp, tp, KV_RANK))
                    state.setdefault("kpe", []).append(kpe.reshape(bp, tp, ROPE_DIM))
                s["x"] = _out_proj_call(s["x"], m, s["rows"], [a, att], w_o)
        else:
            o = l // 2
            w_in = od_w_in[o].astype(BF16)
            w_o = od_w_o[o].astype(BF16)
            w_s = od_w_s[o].astype(BF16)
            b_s_full = jnp.repeat(od_b_s[o].T, CMLP_GROUP_W, axis=1)
            for s in streams:
                m = mods[l, s["conds"]]
                p = _odd_in_call(s["x"], m, s["rows"], g_mix, w_in, od_ln_g[o][None, :], od_ln_b[o][None, :],
                                 w_s, b_s_full)
                s["x"] = _out_proj_call(s["x"], m, s["rows"], [p], w_o)
        for s in streams:
            m = mods[l, s["conds"]]
            s["x"] = _ffn_call(s["x"], m, s["rows"], g_ffn, l, ffn_w_gate, ffn_w_up, ffn_w_down,
                               final_g=final_norm_g[None, :] if last else None)

    y_prompt = streams[0]["x"].reshape(bp, tp, D_MODEL)
    y_sample = streams[1]["x"].reshape(bs, ts, D_MODEL)
    state_ckv = jnp.stack(state["ckv"], axis=1)
    state_kpe = jnp.stack(state["kpe"], axis=1)
    return (y_prompt, y_sample, state_ckv, state_kpe)
```

```python
import functools

import jax
import jax.numpy as jnp
from jax import lax
from jax.experimental import pallas as pl
from jax.experimental.pallas import tpu as pltpu

D_MODEL = 2048
DEPTH = 2
GRID_W = 64
D_CONV = D_MODEL // 2
CONV_WIDTH = 31
CONV_PAD = CONV_WIDTH // 2
MLA_HEADS = D_MODEL // 256
Q_RANK = D_MODEL // 4
KV_RANK = D_MODEL // 8
NOPE_DIM = 128
ROPE_DIM = 64
V_DIM = 128
QK_DIM = NOPE_DIM + ROPE_DIM
ROPE_THETA = 10000.0
D_CMLP = D_MODEL
CHUNK = 128
CMLP_GROUPS = 8
CMLP_GROUP_W = D_CMLP // CMLP_GROUPS
D_FF = 256 * (-(-8 * D_MODEL // (3 * 256)))
ATTN_SCALE = QK_DIM ** -0.5
EXP2_SCALE = ATTN_SCALE * 1.4426950408889634
EPS = 1e-6

LANES = 128
SUBLANES = 8
HEAD_PAD = 2 * LANES
CONV_HALO = 2 * SUBLANES
N_MODS = 8
MIB = 1024 * 1024

BF16 = jnp.bfloat16
F32 = jnp.float32


def _cparams(semantics, vmem_mib):
    return pltpu.CompilerParams(dimension_semantics=semantics, vmem_limit_bytes=vmem_mib * MIB)


def _silu(x):
    return x * (1.0 / (1.0 + jnp.exp(-x)))


def _gelu_tanh(x):
    cdf = 0.5 * (1.0 + jnp.tanh(0.7978845608028654 * (x + 0.044715 * (x * x * x))))
    return x * cdf


def _rms(x, g):
    return (x * lax.rsqrt(jnp.mean(x * x, axis=-1, keepdims=True) + EPS)) * g


def _layernorm(x, g, b):
    mu = jnp.mean(x, axis=-1, keepdims=True)
    xc = x - mu
    var = jnp.mean(xc * xc, axis=-1, keepdims=True)
    return xc * lax.rsqrt(var + EPS) * g + b


def _modnorm(x, g, mods, row):
    return _rms(x, g) * (1.0 + mods[row + 1:row + 2, :]) + mods[row:row + 1, :]


def _dot(a, b):
    return jnp.dot(a, b, preferred_element_type=F32)


NORM_ROWS = 2 * SUBLANES


def _row_loop(n_rows, block, body, unroll=1):
    def step(i, carry):
        body(pl.ds(pl.multiple_of(i * block, block), block))
        return carry

    lax.fori_loop(0, n_rows // block, step, 0, unroll=unroll)


def _fill_rms_scale(n_rows, read, r_ref):
    def body(rows):
        x = read(rows)
        ms = jnp.mean(x * x, axis=-1, keepdims=True)
        r_ref[rows, :] = jnp.broadcast_to(lax.rsqrt(ms + EPS), (SUBLANES, LANES))

    _row_loop(n_rows, SUBLANES, body, unroll=32)


def _mods_kernel(cond_ref, w_ref, b_ref, o_ref):
    s = _silu(cond_ref[...]).astype(BF16)
    o_ref[0] = _dot(s, w_ref[0].astype(BF16)) + b_ref[0]


def _mods_call(cond8, mod_w, mod_b):
    tn = 1024
    n = mod_w.shape[-1]
    return pl.pallas_call(
        _mods_kernel,
        grid=(DEPTH, n // tn),
        in_specs=[
            pl.BlockSpec((N_MODS, D_MODEL), lambda l, j: (0, 0)),
            pl.BlockSpec((1, D_MODEL, tn), lambda l, j: (l, 0, j)),
            pl.BlockSpec((1, 1, tn), lambda l, j: (l, 0, j)),
        ],
        out_specs=pl.BlockSpec((1, N_MODS, tn), lambda l, j: (l, 0, j)),
        out_shape=jax.ShapeDtypeStruct((DEPTH, N_MODS, n), F32),
        compiler_params=_cparams(("arbitrary", "arbitrary"), 32),
        name="mods",
    )(cond8, mod_w, mod_b.reshape(DEPTH, 1, n))


O_GLU = 2 * D_CONV
O_Q = O_GLU + Q_RANK
O_KV = O_Q + KV_RANK
O_END = O_KV + 2 * ROPE_DIM


def _even_in_kernel(rope, x_ref, mods_ref, g_ref, w_ref, gq_ref, gkv_ref, wq_ref, *rest):
    if rope:
        wqs_ref, cq_ref, sq_ref, ck_ref, sk_ref, a_ref, q_ref, ckv_ref, kpe_ref = rest
    else:
        a_ref, q_ref, ckv_ref, kpe_ref = rest
    h = _modnorm(x_ref[...], g_ref[...], mods_ref[0], 0).astype(BF16)
    lin = _dot(h, w_ref[:, :D_CONV])
    gate = _dot(h, w_ref[:, D_CONV:O_GLU])
    a_ref[...] = lin * (1.0 / (1.0 + jnp.exp(-gate)))
    qn = _rms(_dot(h, w_ref[:, O_GLU:O_Q]), gq_ref[...]).astype(BF16)
    ckv_ref[...] = _rms(_dot(h, w_ref[:, O_Q:O_KV]), gkv_ref[...])
    kp = _dot(h, w_ref[:, O_KV:O_END])
    q = _dot(qn, wq_ref[...]) * EXP2_SCALE
    if rope:
        kpe_ref[...] = kp[:, :ROPE_DIM] * ck_ref[...] + kp[:, ROPE_DIM:] * sk_ref[...]
        qs = _dot(qn, wqs_ref[...]) * EXP2_SCALE
        cq = cq_ref[...]
        sq = sq_ref[...]
        for hd in range(MLA_HEADS):
            c0 = hd * HEAD_PAD
            q_ref[:, c0:c0 + LANES] = q[:, c0:c0 + LANES].astype(BF16)
            q_ref[:, c0 + LANES:c0 + HEAD_PAD] = (
                q[:, c0 + LANES:c0 + HEAD_PAD] * cq + qs[:, hd * LANES:(hd + 1) * LANES] * sq).astype(BF16)
    else:
        kpe_ref[...] = kp[:, :ROPE_DIM]
        q_ref[...] = q.astype(BF16)


def _even_in_call(x, mods, rows_per_cond, g_mix, w1, g_qa, g_kva, wq, rope_args, tm=256):
    r = x.shape[0]
    rope = rope_args is not None
    const = lambda i: (0, 0)
    row = lambda i: (i, 0)
    in_specs = [
        pl.BlockSpec((tm, D_MODEL), row),
        pl.BlockSpec((1, N_MODS, D_MODEL), lambda i: ((i * tm) // rows_per_cond, 0, 0)),
        pl.BlockSpec((1, D_MODEL), const),
        pl.BlockSpec(w1.shape, const),
        pl.BlockSpec((1, Q_RANK), const),
        pl.BlockSpec((1, KV_RANK), const),
        pl.BlockSpec(wq.shape, const),
    ]
    args = [x, mods, g_mix, w1, g_qa, g_kva, wq]
    if rope:
        wqs, cq, sq, ck, sk = rope_args
        t_blocks = cq.shape[0] // tm
        pos = lambda i: (i % t_blocks, 0)
        in_specs += [
            pl.BlockSpec(wqs.shape, const),
            pl.BlockSpec((tm, LANES), pos),
            pl.BlockSpec((tm, LANES), pos),
            pl.BlockSpec((tm, ROPE_DIM), pos),
            pl.BlockSpec((tm, ROPE_DIM), pos),
        ]
        args += [wqs, cq, sq, ck, sk]
    return pl.pallas_call(
        functools.partial(_even_in_kernel, rope),
        grid=(r // tm,),
        in_specs=in_specs,
        out_specs=[
            pl.BlockSpec((tm, D_CONV), row),
            pl.BlockSpec((tm, MLA_HEADS * HEAD_PAD), row),
            pl.BlockSpec((tm, KV_RANK), row),
            pl.BlockSpec((tm, ROPE_DIM), row),
        ],
        out_shape=[
            jax.ShapeDtypeStruct((r, D_CONV), F32),
            jax.ShapeDtypeStruct((r, MLA_HEADS * HEAD_PAD), BF16),
            jax.ShapeDtypeStruct((r, KV_RANK), F32),
            jax.ShapeDtypeStruct((r, ROPE_DIM), F32),
        ],
        compiler_params=_cparams(("arbitrary",), 56),
        name="even_in",
    )(*args)


CONV_ROWS = 32
SHIFT_ROWS = 40


def _conv_kernel(tt, n_t, cur_ref, prev_ref, next_ref, w_ref, b_ref, g_ref, beta_ref, o_ref, buf_ref, sh_ref, y_ref):
    i = pl.program_id(1)
    keep_prev = (i > 0).astype(F32)
    keep_next = (i < n_t - 1).astype(F32)
    buf_ref[0:CONV_HALO, :] = prev_ref[0] * keep_prev
    buf_ref[CONV_HALO:CONV_HALO + tt, :] = cur_ref[0]
    buf_ref[CONV_HALO + tt:, :] = next_ref[0] * keep_next
    lane_tiles = [slice(c * LANES, (c + 1) * LANES) for c in range(D_CONV // LANES)]

    def shift(rows):
        for cs in lane_tiles:
            win = buf_ref[pl.ds(rows.start, SHIFT_ROWS + SUBLANES), cs]
            for r in range(1, SUBLANES):
                sh_ref[r - 1, rows, cs] = win[r:r + SHIFT_ROWS, :]

    _row_loop(tt + 2 * CONV_HALO - SUBLANES, SHIFT_ROWS, shift)

    first = CONV_HALO - CONV_PAD
    n_win = CONV_ROWS + (first + CONV_WIDTH - 1) // SUBLANES * SUBLANES

    def taps(rows):
        win_rows = pl.ds(rows.start, n_win)
        for cs in lane_tiles:
            acc = b_ref[:, cs]
            for r in range(SUBLANES):
                win = buf_ref[win_rows, cs] if r == 0 else sh_ref[r - 1, win_rows, cs]
                for q in range(n_win // SUBLANES):
                    k = q * SUBLANES + r - first
                    if 0 <= k < CONV_WIDTH:
                        acc = acc + win[q * SUBLANES:q * SUBLANES + CONV_ROWS, :] * w_ref[k:k + 1, cs]
            y_ref[rows, cs] = acc

    _row_loop(tt, CONV_ROWS, taps)
    o_ref[0] = _silu(_layernorm(y_ref[...], g_ref[...], beta_ref[...])).astype(BF16)


def _conv_call(a, batch, t, conv_w, conv_b, ln_g, ln_b, tt):
    n_t = t // tt
    assert tt % CONV_ROWS == 0 and (tt + 2 * CONV_HALO - SUBLANES) % SHIFT_ROWS == 0
    a3 = a.reshape(batch, t, D_CONV)
    hb = tt // CONV_HALO
    n_hb = t // CONV_HALO
    const = lambda b, i: (0, 0)
    out = pl.pallas_call(
        functools.partial(_conv_kernel, tt, n_t),
        grid=(batch, n_t),
        in_specs=[
            pl.BlockSpec((1, tt, D_CONV), lambda b, i: (b, i, 0)),
            pl.BlockSpec((1, CONV_HALO, D_CONV), lambda b, i: (b, jnp.maximum(i * hb - 1, 0), 0)),
            pl.BlockSpec((1, CONV_HALO, D_CONV), lambda b, i: (b, jnp.minimum((i + 1) * hb, n_hb - 1), 0)),
            pl.BlockSpec((CONV_WIDTH, D_CONV), const),
            pl.BlockSpec((1, D_CONV), const),
            pl.BlockSpec((1, D_CONV), const),
            pl.BlockSpec((1, D_CONV), const),
        ],
        out_specs=pl.BlockSpec((1, tt, D_CONV), lambda b, i: (b, i, 0)),
        out_shape=jax.ShapeDtypeStruct((batch, t, D_CONV), BF16),
        scratch_shapes=[
            pltpu.VMEM((tt + 2 * CONV_HALO, D_CONV), F32),
            pltpu.VMEM((SUBLANES - 1, tt + 2 * CONV_HALO - SUBLANES, D_CONV), F32),
            pltpu.VMEM((tt, D_CONV), F32),
        ],
        compiler_params=_cparams(("arbitrary", "arbitrary"), 32),
        name="conv",
    )(a3, a3, a3, conv_w, conv_b, ln_g, ln_b)
    return out.reshape(batch * t, D_CONV)


KV_ROWS = 1024
Q_TILE = 256
KEY_PIECE = 512
NT_DIMS = (((1,), (1,)), ((), ()))


def _rope_key_pad(kpe):
    return jnp.concatenate([kpe, jnp.zeros((kpe.shape[0], HEAD_PAD - QK_DIM), F32)], axis=-1).astype(BF16)


def _fill_kv(k_ref, vt_ref, row0, ckv, kpe, wk, wvt):
    c = ckv.astype(BF16)
    n = c.shape[0]
    k_ref[row0:row0 + n, 0:NOPE_DIM] = _dot(c, wk).astype(BF16)
    k_ref[row0:row0 + n, NOPE_DIM:HEAD_PAD] = _rope_key_pad(kpe)
    vt_ref[:, row0:row0 + n] = lax.dot_general(wvt, c, NT_DIMS, preferred_element_type=F32).astype(BF16)


def _attn_kernel(t, past, q_ref, ckv_ref, kpe_ref, wk_ref, wvt_ref, cckv_ref, ckpe_ref, o_ref, k_ref, vt_ref, s_ref):
    @pl.when(pl.program_id(2) == 0)
    def _():
        wk = wk_ref[...]
        wvt = wvt_ref[...]
        _fill_kv(k_ref, vt_ref, 0, cckv_ref[0], ckpe_ref[0], wk, wvt)
        for r0 in range(0, t, KV_ROWS):
            _fill_kv(k_ref, vt_ref, past + r0, ckv_ref[r0:r0 + KV_ROWS, :], kpe_ref[r0:r0 + KV_ROWS, :], wk, wvt)

    n_tiles = q_ref.shape[0] // Q_TILE
    tk = k_ref.shape[0]
    pieces = [slice(k0, min(k0 + KEY_PIECE, tk)) for k0 in range(0, tk, KEY_PIECE)]

    def add(a, b):
        return b if a is None else a + b

    def score_piece(j, p, mx):
        q = q_ref[j * Q_TILE:(j + 1) * Q_TILE, :]
        s_ref[j % 2, pieces[p], :] = lax.dot_general(
            k_ref[pieces[p], :], q, NT_DIMS, preferred_element_type=F32)
        pm = jnp.max(s_ref[j % 2, pieces[p], :], axis=0, keepdims=True)
        return pm if mx is None else jnp.maximum(mx, pm)

    def exp_piece(j, p, m):
        e = jnp.exp2(s_ref[j % 2, pieces[p], :] - m)
        return e.astype(BF16), jnp.sum(e, axis=0, keepdims=True)

    def value_piece(p, e):
        return _dot(vt_ref[:, pieces[p]], e)

    m_next = None
    for p in range(len(pieces)):
        m_next = score_piece(0, p, m_next)
    for j in range(n_tiles):
        m, m_next = m_next, None
        acc = l = e_prev = None
        for p in range(len(pieces)):
            if j + 1 < n_tiles:
                m_next = score_piece(j + 1, p, m_next)
            if e_prev is not None:
                acc = add(acc, value_piece(p - 1, e_prev))
            e_prev, ls = exp_piece(j, p, m)
            l = add(l, ls)
        acc = add(acc, value_piece(len(pieces) - 1, e_prev))
        o_ref[j * Q_TILE:(j + 1) * Q_TILE, :] = (acc * (1.0 / l)).T.astype(BF16)


def _attn_call(q, ckv, kpe, wkv, wkv_t, batch, t, cache, tq):
    nq = t // tq
    past = cache[0].shape[1]
    tk = past + t
    assert t % KV_ROWS == 0 and past % LANES == 0 and tq % Q_TILE == 0
    return pl.pallas_call(
        functools.partial(_attn_kernel, t, past),
        grid=(batch, MLA_HEADS, nq),
        in_specs=[
            pl.BlockSpec((tq, HEAD_PAD), lambda b, h, i: (b * nq + i, h)),
            pl.BlockSpec((t, KV_RANK), lambda b, h, i: (b, 0)),
            pl.BlockSpec((t, ROPE_DIM), lambda b, h, i: (b, 0)),
            pl.BlockSpec((KV_RANK, NOPE_DIM), lambda b, h, i: (0, 2 * h)),
            pl.BlockSpec((V_DIM, KV_RANK), lambda b, h, i: (2 * h + 1, 0)),
            pl.BlockSpec((1, past, KV_RANK), lambda b, h, i: (b, 0, 0)),
            pl.BlockSpec((1, past, ROPE_DIM), lambda b, h, i: (b, 0, 0)),
        ],
        out_specs=pl.BlockSpec((tq, V_DIM), lambda b, h, i: (b * nq + i, h)),
        out_shape=jax.ShapeDtypeStruct((batch * t, MLA_HEADS * V_DIM), BF16),
        scratch_shapes=[
            pltpu.VMEM((tk, HEAD_PAD), BF16),
            pltpu.VMEM((V_DIM, tk), BF16),
            pltpu.VMEM((2, tk, Q_TILE), F32),
        ],
        compiler_params=_cparams(("arbitrary", "arbitrary", "arbitrary"), 56),
        name="attention",
    )(q, ckv, kpe, wkv, wkv_t, *cache)


def _attn_seq_kernel(q_ref, ckv_ref, kpe_ref, wkv_ref, o_ref):
    kv = _dot(ckv_ref[...].astype(BF16), wkv_ref[...])
    kpad = _rope_key_pad(kpe_ref[...])
    for hd in range(MLA_HEADS):
        c0 = hd * (NOPE_DIM + V_DIM)
        k = jnp.concatenate([kv[:, c0:c0 + NOPE_DIM].astype(BF16), kpad], axis=-1)
        s = lax.dot_general(q_ref[:, hd * HEAD_PAD:(hd + 1) * HEAD_PAD], k, NT_DIMS, preferred_element_type=F32)
        m = jnp.max(s, axis=-1, keepdims=True)
        e = jnp.exp2(s - m)
        l = jnp.sum(e, axis=-1, keepdims=True)
        o = _dot(e.astype(BF16), kv[:, c0 + NOPE_DIM:c0 + NOPE_DIM + V_DIM].astype(BF16)) * (1.0 / l)
        o_ref[:, hd * V_DIM:(hd + 1) * V_DIM] = o.astype(BF16)


def _attn_seq_call(q, ckv, kpe, wkv, batch, t):
    row = lambda b: (b, 0)
    return pl.pallas_call(
        _attn_seq_kernel,
        grid=(batch,),
        in_specs=[
            pl.BlockSpec((t, MLA_HEADS * HEAD_PAD), row),
            pl.BlockSpec((t, KV_RANK), row),
            pl.BlockSpec((t, ROPE_DIM), row),
            pl.BlockSpec(wkv.shape, lambda b: (0, 0)),
        ],
        out_specs=pl.BlockSpec((t, MLA_HEADS * V_DIM), row),
        out_shape=jax.ShapeDtypeStruct((batch * t, MLA_HEADS * V_DIM), BF16),
        compiler_params=_cparams(("arbitrary",), 32),
        name="attention_seq",
    )(q, ckv, kpe, wkv)


def _out_proj_kernel(n_in, x_ref, mods_ref, *rest):
    in_refs = rest[:n_in]
    w_ref, o_ref = rest[n_in:]
    acc = None
    k0 = 0
    for r in in_refs:
        k = r.shape[-1]
        part = _dot(r[...], w_ref[k0:k0 + k, :])
        acc = part if acc is None else acc + part
        k0 += k
    o_ref[...] = x_ref[...] + mods_ref[0][2:3, :] * acc


def _out_proj_call(x, mods, rows_per_cond, parts, w_o, tm=512):
    r = x.shape[0]
    row = lambda i: (i, 0)
    in_specs = [
        pl.BlockSpec((tm, D_MODEL), row),
        pl.BlockSpec((1, N_MODS, D_MODEL), lambda i: ((i * tm) // rows_per_cond, 0, 0)),
    ]
    in_specs += [pl.BlockSpec((tm, p.shape[-1]), row) for p in parts]
    in_specs += [pl.BlockSpec(w_o.shape, lambda i: (0, 0))]
    return pl.pallas_call(
        functools.partial(_out_proj_kernel, len(parts)),
        grid=(r // tm,),
        in_specs=in_specs,
        out_specs=pl.BlockSpec((tm, D_MODEL), row),
        out_shape=jax.ShapeDtypeStruct((r, D_MODEL), F32),
        compiler_params=_cparams(("arbitrary",), 48),
        name="out_proj",
    )(x, mods, *parts, w_o)


def _odd_in_kernel(tm, x_ref, mods_ref, g_ref, w_ref, lg_ref, lb_ref, ws_ref, bs_ref, o_ref):
    h = _modnorm(x_ref[...], g_ref[...], mods_ref[0], 0).astype(BF16)
    v = _gelu_tanh(_dot(h, w_ref[:, D_CMLP:]))
    v = _layernorm(v, lg_ref[...], lb_ref[...]).astype(BF16)
    u = _gelu_tanh(_dot(h, w_ref[:, :D_CMLP]))
    for ch in range(tm // CHUNK):
        rs = slice(ch * CHUNK, (ch + 1) * CHUNK)
        for g in range(CMLP_GROUPS):
            cs = slice(g * CMLP_GROUP_W, (g + 1) * CMLP_GROUP_W)
            mixed = _dot(ws_ref[g], v[rs, cs]) + bs_ref[:, cs]
            o_ref[rs, cs] = (u[rs, cs] * mixed).astype(BF16)


def _odd_in_call(x, mods, rows_per_cond, g_mix, w_in, ln_g, ln_b, w_s, b_s_full, tm=256):
    r = x.shape[0]
    const = lambda i: (0, 0)
    row = lambda i: (i, 0)
    return pl.pallas_call(
        functools.partial(_odd_in_kernel, tm),
        grid=(r // tm,),
        in_specs=[
            pl.BlockSpec((tm, D_MODEL), row),
            pl.BlockSpec((1, N_MODS, D_MODEL), lambda i: ((i * tm) // rows_per_cond, 0, 0)),
            pl.BlockSpec((1, D_MODEL), const),
            pl.BlockSpec(w_in.shape, const, pipeline_mode=pl.Buffered(1)),
            pl.BlockSpec((1, D_CMLP), const),
            pl.BlockSpec((1, D_CMLP), const),
            pl.BlockSpec(w_s.shape, lambda i: (0, 0, 0)),
            pl.BlockSpec((CHUNK, D_CMLP), const),
        ],
        out_specs=pl.BlockSpec((tm, D_CMLP), row),
        out_shape=jax.ShapeDtypeStruct((r, D_CMLP), BF16),
        compiler_params=_cparams(("arbitrary",), 56),
        name="odd_in",
    )(x, mods, g_mix, w_in, ln_g, ln_b, w_s, b_s_full)


def _ffn_kernel(final, x_ref, mods_ref, g_ref, wg_ref, wu_ref, wd_ref, *rest):
    if final:
        fg_ref, o_ref, h_ref, gs_ref, r_ref = rest
    else:
        o_ref, h_ref, gs_ref, r_ref = rest
    j = pl.program_id(1)
    tm = x_ref.shape[0]
    lane_tiles = [slice(c * LANES, (c + 1) * LANES) for c in range(D_MODEL // LANES)]

    @pl.when(j == 0)
    def _():
        gs_ref[0:1, :] = g_ref[...] * (1.0 + mods_ref[0][4:5, :])
        gs_ref[1:2, :] = mods_ref[0][3:4, :]
        _fill_rms_scale(tm, lambda rows: x_ref[rows, :], r_ref)

        def prologue(rows):
            r = r_ref[rows, :]
            for cs in lane_tiles:
                h_ref[rows, cs] = (x_ref[rows, cs] * r * gs_ref[0:1, cs] + gs_ref[1:2, cs]).astype(BF16)

        _row_loop(tm, NORM_ROWS, prologue, unroll=2)

    def down_proj():
        h = h_ref[...]
        hid = (_silu(_dot(h, wg_ref[...].astype(BF16))) * _dot(h, wu_ref[...].astype(BF16))).astype(BF16)
        return _dot(hid, wd_ref[...].astype(BF16))

    @pl.when(j == 0)
    def _():
        o_ref[...] = down_proj()

    @pl.when(j > 0)
    def _():
        o_ref[...] += down_proj()

    @pl.when(j == pl.num_programs(1) - 1)
    def _():
        o_ref[...] = x_ref[...] + mods_ref[0][5:6, :] * o_ref[...]
        if final:
            _fill_rms_scale(tm, lambda rows: o_ref[rows, :], r_ref)

            def final_norm(rows):
                r = r_ref[rows, :]
                for cs in lane_tiles:
                    o_ref[rows, cs] = o_ref[rows, cs] * r * fg_ref[:, cs]

            _row_loop(tm, NORM_ROWS, final_norm, unroll=2)


def _ffn_call(x, mods, rows_per_cond, g_ffn, layer, wg, wu, wd, final_g=None, tm=1024, tn=256):
    r = x.shape[0]
    final = final_g is not None
    const = lambda i, j: (0, 0)
    row = lambda i, j: (i, 0)
    in_specs = [
        pl.BlockSpec((tm, D_MODEL), row),
        pl.BlockSpec((1, N_MODS, D_MODEL), lambda i, j: ((i * tm) // rows_per_cond, 0, 0)),
        pl.BlockSpec((1, D_MODEL), const),
        pl.BlockSpec((None, D_MODEL, tn), lambda i, j: (layer, 0, j)),
        pl.BlockSpec((None, D_MODEL, tn), lambda i, j: (layer, 0, j)),
        pl.BlockSpec((None, tn, D_MODEL), lambda i, j: (layer, j, 0)),
    ]
    args = [x, mods, g_ffn, wg, wu, wd]
    if final:
        in_specs.append(pl.BlockSpec((1, D_MODEL), const))
        args.append(final_g)
    return pl.pallas_call(
        functools.partial(_ffn_kernel, final),
        grid=(r // tm, D_FF // tn),
        in_specs=in_specs,
        out_specs=pl.BlockSpec((tm, D_MODEL), row),
        out_shape=jax.ShapeDtypeStruct((r, D_MODEL), F32),
        scratch_shapes=[
            pltpu.VMEM((tm, D_MODEL), BF16),
            pltpu.VMEM((SUBLANES, D_MODEL), F32),
            pltpu.VMEM((tm, LANES), F32),
        ],
        compiler_params=_cparams(("arbitrary", "arbitrary"), 60),
        name="ffn",
    )(*args)


def _rope_tables(n_tokens):
    n_freq = ROPE_DIM // 4
    lane = jnp.arange(ROPE_DIM)
    tok = jnp.arange(n_tokens)
    pos = jnp.where(lane[None, :] // (2 * n_freq) == 0, tok[:, None] // GRID_W, tok[:, None] % GRID_W).astype(F32)
    inv_freq = jnp.power(ROPE_THETA, -jnp.arange(n_freq, dtype=F32) * 2.0 / (ROPE_DIM // 2))
    ang = pos * inv_freq[lane % n_freq][None, :]
    sign = jnp.where((lane // n_freq) % 2 == 0, -1.0, 1.0).astype(F32)
    return jnp.cos(ang), jnp.sin(ang) * sign[None, :]


def _pair_swap(w):
    perm = jnp.arange(ROPE_DIM) ^ (ROPE_DIM // 4)
    return w[..., perm]


def kernel(x_prompt, x_sample, cache_ckv, cache_kpe, c, c_ctx, mod_w, mod_b, norm_mix_g, norm_ffn_g, ffn_w_gate, ffn_w_up, ffn_w_down, ev_w_in, ev_conv_w, ev_conv_b, ev_conv_ln_g, ev_conv_ln_b, ev_q_norm_g, ev_w_qb, ev_kv_norm_g, ev_w_kvb, ev_w_o, od_w_in, od_ln_g, od_ln_b, od_w_s, od_b_s, od_w_o, final_norm_g):
    bp, tp, _ = x_prompt.shape
    bs, ts, _ = x_sample.shape
    xp = x_prompt.reshape(bp * tp, D_MODEL)
    xs = x_sample.reshape(bs * ts, D_MODEL)

    cond8 = jnp.concatenate([c_ctx[None, :], c, jnp.zeros((N_MODS - 1 - bs, D_MODEL), F32)], axis=0)
    mods = _mods_call(cond8, mod_w, mod_b)
    mods = mods[:, :1 + bs].reshape(DEPTH, 1 + bs, 6, D_MODEL)
    mods = jnp.pad(mods, ((0, 0), (0, 0), (0, N_MODS - 6), (0, 0)))

    cos_k, sin_k = _rope_tables(ts)
    zpad = jnp.zeros((ts, LANES - ROPE_DIM), F32)
    cos_q = jnp.concatenate([cos_k, zpad], axis=-1)
    sin_q = jnp.concatenate([sin_k, zpad], axis=-1)

    streams = [
        dict(x=xp, batch=bp, t=tp, rows=bp * tp, conds=slice(0, 1), rope=False),
        dict(x=xs, batch=bs, t=ts, rows=ts, conds=slice(1, 1 + bs), rope=True),
    ]
    state = {}
    for l in range(DEPTH):
        g_mix = norm_mix_g[l][None, :]
        g_ffn = norm_ffn_g[l][None, :]
        last = l == DEPTH - 1
        if l % 2 == 0:
            e = l // 2
            w_in = ev_w_in[e]
            w1 = jnp.concatenate([w_in, _pair_swap(w_in[:, O_KV:])], axis=1).astype(BF16)
            wq3 = ev_w_qb[e].reshape(Q_RANK, MLA_HEADS, QK_DIM)
            wq = jnp.pad(wq3, ((0, 0), (0, 0), (0, HEAD_PAD - QK_DIM))).reshape(Q_RANK, -1).astype(BF16)
            wqs = jnp.pad(_pair_swap(wq3[:, :, NOPE_DIM:]), ((0, 0), (0, 0), (0, LANES - ROPE_DIM)))
            wqs = wqs.reshape(Q_RANK, -1).astype(BF16)
            wkv = ev_w_kvb[e].astype(BF16)
            wkv_t = ev_w_kvb[e].T.astype(BF16)
            w_o = ev_w_o[e].astype(BF16)
            conv_w = ev_conv_w[e].reshape(CONV_WIDTH, D_CONV)
            for s in streams:
                m = mods[l, s["conds"]]
                rope_args = (wqs, cos_q, sin_q, cos_k, sin_k) if s["rope"] else None
                a, q, ckv, kpe = _even_in_call(s["x"], m, s["rows"], g_mix, w1, ev_q_norm_g[e][None, :],
                                               ev_kv_norm_g[e][None, :], wq, rope_args)
                a = _conv_call(a, s["batch"], s["t"], conv_w, ev_conv_b[e][None, :],
                               ev_conv_ln_g[e][None, :], ev_conv_ln_b[e][None, :], tt=min(s["t"], 256))
                if s["rope"]:
                    att = _attn_call(q, ckv, kpe, wkv, wkv_t, s["batch"], s["t"],
                                     (cache_ckv[:, e], cache_kpe[:, e]), tq=2048)
                else:
                    att = _attn_seq_call(q, ckv, kpe, wkv, s["batch"], s["t"])
                    state.setdefault("ckv", []).append(ckv.reshape(bp, tp, KV_RANK))
                    state.setdefault("kpe", []).append(kpe.reshape(bp, tp, ROPE_DIM))
                s["x"] = _out_proj_call(s["x"], m, s["rows"], [a, att], w_o)
        else:
            o = l // 2
            w_in = od_w_in[o].astype(BF16)
            w_o = od_w_o[o].astype(BF16)
            w_s = od_w_s[o].astype(BF16)
            b_s_full = jnp.repeat(od_b_s[o].T, CMLP_GROUP_W, axis=1)
            for s in streams:
                m = mods[l, s["conds"]]
                p = _odd_in_call(s["x"], m, s["rows"], g_mix, w_in, od_ln_g[o][None, :], od_ln_b[o][None, :],
                                 w_s, b_s_full)
                s["x"] = _out_proj_call(s["x"], m, s["rows"], [p], w_o)
        for s in streams:
            m = mods[l, s["conds"]]
            s["x"] = _ffn_call(s["x"], m, s["rows"], g_ffn, l, ffn_w_gate, ffn_w_up, ffn_w_down,
                               final_g=final_norm_g[None, :] if last else None)

    y_prompt = streams[0]["x"].reshape(bp, tp, D_MODEL)
    y_sample = streams[1]["x"].reshape(bs, ts, D_MODEL)
    state_ckv = jnp.stack(state["ckv"], axis=1)
    state_kpe = jnp.stack(state["kpe"], axis=1)
    return (y_prompt, y_sample, state_ckv, state_kpe)
```

```python
import functools

import jax
import jax.numpy as jnp
import numpy as np
from jax import lax
from jax.experimental import pallas as pl
from jax.experimental.pallas import tpu as pltpu

D_MODEL = 2048
DEPTH = 2
GRID_W = 64
D_CONV = D_MODEL // 2
CONV_WIDTH = 31
CONV_PAD = CONV_WIDTH // 2
MLA_HEADS = D_MODEL // 256
Q_RANK = D_MODEL // 4
KV_RANK = D_MODEL // 8
NOPE_DIM = 128
ROPE_DIM = 64
V_DIM = 128
QK_DIM = NOPE_DIM + ROPE_DIM
ROPE_THETA = 10000.0
D_CMLP = D_MODEL
CHUNK = 128
CMLP_GROUPS = 8
CMLP_GROUP_W = D_CMLP // CMLP_GROUPS
D_FF = 256 * (-(-8 * D_MODEL // (3 * 256)))
ATTN_SCALE = QK_DIM ** -0.5
EXP2_SCALE = ATTN_SCALE * 1.4426950408889634
EPS = 1e-6

LANES = 128
SUBLANES = 8
HEAD_PAD = 2 * LANES
CONV_HALO = 2 * SUBLANES
N_MODS = 8
MIB = 1024 * 1024

BF16 = jnp.bfloat16
F32 = jnp.float32


def _cparams(semantics, vmem_mib):
    return pltpu.CompilerParams(dimension_semantics=semantics, vmem_limit_bytes=vmem_mib * MIB)


def _silu(x):
    return x * (1.0 / (1.0 + jnp.exp(-x)))


def _gelu_tanh(x):
    cdf = 0.5 * (1.0 + jnp.tanh(0.7978845608028654 * (x + 0.044715 * (x * x * x))))
    return x * cdf


def _rms(x, g):
    return (x * lax.rsqrt(jnp.mean(x * x, axis=-1, keepdims=True) + EPS)) * g


def _layernorm(x, g, b):
    mu = jnp.mean(x, axis=-1, keepdims=True)
    xc = x - mu
    var = jnp.mean(xc * xc, axis=-1, keepdims=True)
    return xc * lax.rsqrt(var + EPS) * g + b


def _modnorm(x, g, mods, row):
    return _rms(x, g) * (1.0 + mods[row + 1:row + 2, :]) + mods[row:row + 1, :]


def _dot(a, b):
    return jnp.dot(a, b, preferred_element_type=F32)


NORM_ROWS = 2 * SUBLANES


def _row_loop(n_rows, block, body, unroll=1):
    def step(i, carry):
        body(pl.ds(pl.multiple_of(i * block, block), block))
        return carry

    lax.fori_loop(0, n_rows // block, step, 0, unroll=unroll)


def _fill_rms_scale(n_rows, read, r_ref):
    def body(rows):
        x = read(rows)
        ms = jnp.mean(x * x, axis=-1, keepdims=True)
        r_ref[rows, :] = jnp.broadcast_to(lax.rsqrt(ms + EPS), (SUBLANES, LANES))

    _row_loop(n_rows, SUBLANES, body, unroll=32)


def _mods_kernel(cond_ref, w_ref, b_ref, o_ref):
    s = _silu(cond_ref[...]).astype(BF16)
    o_ref[0] = _dot(s, w_ref[0].astype(BF16)) + b_ref[0]


def _mods_call(cond8, mod_w, mod_b):
    tn = 1024
    n = mod_w.shape[-1]
    return pl.pallas_call(
        _mods_kernel,
        grid=(DEPTH, n // tn),
        in_specs=[
            pl.BlockSpec((N_MODS, D_MODEL), lambda l, j: (0, 0)),
            pl.BlockSpec((1, D_MODEL, tn), lambda l, j: (l, 0, j)),
            pl.BlockSpec((1, 1, tn), lambda l, j: (l, 0, j)),
        ],
        out_specs=pl.BlockSpec((1, N_MODS, tn), lambda l, j: (l, 0, j)),
        out_shape=jax.ShapeDtypeStruct((DEPTH, N_MODS, n), F32),
        compiler_params=_cparams(("arbitrary", "arbitrary"), 32),
        name="mods",
    )(cond8, mod_w, mod_b.reshape(DEPTH, 1, n))


O_GLU = 2 * D_CONV
O_Q = O_GLU + Q_RANK
O_KV = O_Q + KV_RANK
O_END = O_KV + 2 * ROPE_DIM


def _even_in_kernel(rope, x_ref, mods_ref, g_ref, w_ref, gq_ref, gkv_ref, wq_ref, *rest):
    if rope:
        wqs_ref, cq_ref, sq_ref, ck_ref, sk_ref, a_ref, q_ref, ckv_ref, kpe_ref = rest
    else:
        a_ref, q_ref, ckv_ref, kpe_ref = rest
    h = _modnorm(x_ref[...], g_ref[...], mods_ref[0], 0).astype(BF16)
    lin = _dot(h, w_ref[:, :D_CONV])
    gate = _dot(h, w_ref[:, D_CONV:O_GLU])
    a_ref[...] = lin * (1.0 / (1.0 + jnp.exp(-gate)))
    qn = _rms(_dot(h, w_ref[:, O_GLU:O_Q]), gq_ref[...]).astype(BF16)
    ckv_ref[...] = _rms(_dot(h, w_ref[:, O_Q:O_KV]), gkv_ref[...])
    kp = _dot(h, w_ref[:, O_KV:O_END])
    q = _dot(qn, wq_ref[...]) * EXP2_SCALE
    if rope:
        kpe_ref[...] = kp[:, :ROPE_DIM] * ck_ref[...] + kp[:, ROPE_DIM:] * sk_ref[...]
        qs = _dot(qn, wqs_ref[...]) * EXP2_SCALE
        cq = cq_ref[...]
        sq = sq_ref[...]
        for hd in range(MLA_HEADS):
            c0 = hd * HEAD_PAD
            q_ref[:, c0:c0 + LANES] = q[:, c0:c0 + LANES].astype(BF16)
            q_ref[:, c0 + LANES:c0 + HEAD_PAD] = (
                q[:, c0 + LANES:c0 + HEAD_PAD] * cq + qs[:, hd * LANES:(hd + 1) * LANES] * sq).astype(BF16)
    else:
        kpe_ref[...] = kp[:, :ROPE_DIM]
        q_ref[...] = q.astype(BF16)


def _even_in_call(x, mods, rows_per_cond, g_mix, w1, g_qa, g_kva, wq, rope_args, tm=256):
    r = x.shape[0]
    rope = rope_args is not None
    const = lambda i: (0, 0)
    row = lambda i: (i, 0)
    in_specs = [
        pl.BlockSpec((tm, D_MODEL), row),
        pl.BlockSpec((1, N_MODS, D_MODEL), lambda i: ((i * tm) // rows_per_cond, 0, 0)),
        pl.BlockSpec((1, D_MODEL), const),
        pl.BlockSpec(w1.shape, const),
        pl.BlockSpec((1, Q_RANK), const),
        pl.BlockSpec((1, KV_RANK), const),
        pl.BlockSpec(wq.shape, const),
    ]
    args = [x, mods, g_mix, w1, g_qa, g_kva, wq]
    if rope:
        wqs, cq, sq, ck, sk = rope_args
        t_blocks = cq.shape[0] // tm
        pos = lambda i: (i % t_blocks, 0)
        in_specs += [
            pl.BlockSpec(wqs.shape, const),
            pl.BlockSpec((tm, LANES), pos),
            pl.BlockSpec((tm, LANES), pos),
            pl.BlockSpec((tm, ROPE_DIM), pos),
            pl.BlockSpec((tm, ROPE_DIM), pos),
        ]
        args += [wqs, cq, sq, ck, sk]
    return pl.pallas_call(
        functools.partial(_even_in_kernel, rope),
        grid=(r // tm,),
        in_specs=in_specs,
        out_specs=[
            pl.BlockSpec((tm, D_CONV), row),
            pl.BlockSpec((tm, MLA_HEADS * HEAD_PAD), row),
            pl.BlockSpec((tm, KV_RANK), row),
            pl.BlockSpec((tm, ROPE_DIM), row),
        ],
        out_shape=[
            jax.ShapeDtypeStruct((r, D_CONV), F32),
            jax.ShapeDtypeStruct((r, MLA_HEADS * HEAD_PAD), BF16),
            jax.ShapeDtypeStruct((r, KV_RANK), F32),
            jax.ShapeDtypeStruct((r, ROPE_DIM), F32),
        ],
        compiler_params=_cparams(("arbitrary",), 56),
        name="even_in",
    )(*args)


CONV_ROWS = 32
SHIFT_ROWS = 40


def _conv_kernel(tt, n_t, cur_ref, prev_ref, next_ref, w_ref, b_ref, g_ref, beta_ref, o_ref, buf_ref, sh_ref, y_ref):
    i = pl.program_id(1)
    keep_prev = (i > 0).astype(F32)
    keep_next = (i < n_t - 1).astype(F32)
    buf_ref[0:CONV_HALO, :] = prev_ref[0] * keep_prev
    buf_ref[CONV_HALO:CONV_HALO + tt, :] = cur_ref[0]
    buf_ref[CONV_HALO + tt:, :] = next_ref[0] * keep_next
    lane_tiles = [slice(c * LANES, (c + 1) * LANES) for c in range(D_CONV // LANES)]

    def shift(rows):
        for cs in lane_tiles:
            win = buf_ref[pl.ds(rows.start, SHIFT_ROWS + SUBLANES), cs]
            for r in range(1, SUBLANES):
                sh_ref[r - 1, rows, cs] = win[r:r + SHIFT_ROWS, :]

    _row_loop(tt + 2 * CONV_HALO - SUBLANES, SHIFT_ROWS, shift)

    first = CONV_HALO - CONV_PAD
    n_win = CONV_ROWS + (first + CONV_WIDTH - 1) // SUBLANES * SUBLANES

    def taps(rows):
        win_rows = pl.ds(rows.start, n_win)
        for cs in lane_tiles:
            acc = b_ref[:, cs]
            for r in range(SUBLANES):
                win = buf_ref[win_rows, cs] if r == 0 else sh_ref[r - 1, win_rows, cs]
                for q in range(n_win // SUBLANES):
                    k = q * SUBLANES + r - first
                    if 0 <= k < CONV_WIDTH:
                        acc = acc + win[q * SUBLANES:q * SUBLANES + CONV_ROWS, :] * w_ref[k:k + 1, cs]
            y_ref[rows, cs] = acc

    _row_loop(tt, CONV_ROWS, taps)
    o_ref[0] = _silu(_layernorm(y_ref[...], g_ref[...], beta_ref[...])).astype(BF16)


def _conv_call(a, batch, t, conv_w, conv_b, ln_g, ln_b, tt):
    n_t = t // tt
    assert tt % CONV_ROWS == 0 and (tt + 2 * CONV_HALO - SUBLANES) % SHIFT_ROWS == 0
    a3 = a.reshape(batch, t, D_CONV)
    hb = tt // CONV_HALO
    n_hb = t // CONV_HALO
    const = lambda b, i: (0, 0)
    out = pl.pallas_call(
        functools.partial(_conv_kernel, tt, n_t),
        grid=(batch, n_t),
        in_specs=[
            pl.BlockSpec((1, tt, D_CONV), lambda b, i: (b, i, 0)),
            pl.BlockSpec((1, CONV_HALO, D_CONV), lambda b, i: (b, jnp.maximum(i * hb - 1, 0), 0)),
            pl.BlockSpec((1, CONV_HALO, D_CONV), lambda b, i: (b, jnp.minimum((i + 1) * hb, n_hb - 1), 0)),
            pl.BlockSpec((CONV_WIDTH, D_CONV), const),
            pl.BlockSpec((1, D_CONV), const),
            pl.BlockSpec((1, D_CONV), const),
            pl.BlockSpec((1, D_CONV), const),
        ],
        out_specs=pl.BlockSpec((1, tt, D_CONV), lambda b, i: (b, i, 0)),
        out_shape=jax.ShapeDtypeStruct((batch, t, D_CONV), BF16),
        scratch_shapes=[
            pltpu.VMEM((tt + 2 * CONV_HALO, D_CONV), F32),
            pltpu.VMEM((SUBLANES - 1, tt + 2 * CONV_HALO - SUBLANES, D_CONV), F32),
            pltpu.VMEM((tt, D_CONV), F32),
        ],
        compiler_params=_cparams(("arbitrary", "arbitrary"), 32),
        name="conv",
    )(a3, a3, a3, conv_w, conv_b, ln_g, ln_b)
    return out.reshape(batch * t, D_CONV)


KV_ROWS = 1024
Q_TILE = 256
KEY_PIECE = 512
NT_DIMS = (((1,), (1,)), ((), ()))


def _rope_key_pad(kpe):
    return jnp.concatenate([kpe, jnp.zeros((kpe.shape[0], HEAD_PAD - QK_DIM), F32)], axis=-1).astype(BF16)


def _fill_kv(k_ref, vt_ref, row0, ckv, kpe, wk, wvt):
    c = ckv.astype(BF16)
    n = c.shape[0]
    k_ref[row0:row0 + n, 0:NOPE_DIM] = _dot(c, wk).astype(BF16)
    k_ref[row0:row0 + n, NOPE_DIM:HEAD_PAD] = _rope_key_pad(kpe)
    vt_ref[:, row0:row0 + n] = lax.dot_general(wvt, c, NT_DIMS, preferred_element_type=F32).astype(BF16)


def _attn_kernel(t, past, q_ref, ckv_ref, kpe_ref, wk_ref, wvt_ref, cckv_ref, ckpe_ref, o_ref, k_ref, vt_ref, s_ref):
    @pl.when(pl.program_id(2) == 0)
    def _():
        wk = wk_ref[...]
        wvt = wvt_ref[...]
        _fill_kv(k_ref, vt_ref, 0, cckv_ref[0], ckpe_ref[0], wk, wvt)
        for r0 in range(0, t, KV_ROWS):
            _fill_kv(k_ref, vt_ref, past + r0, ckv_ref[r0:r0 + KV_ROWS, :], kpe_ref[r0:r0 + KV_ROWS, :], wk, wvt)

    n_tiles = q_ref.shape[0] // Q_TILE
    tk = k_ref.shape[0]
    pieces = [slice(k0, min(k0 + KEY_PIECE, tk)) for k0 in range(0, tk, KEY_PIECE)]

    def add(a, b):
        return b if a is None else a + b

    def score_piece(j, p, mx):
        q = q_ref[j * Q_TILE:(j + 1) * Q_TILE, :]
        s_ref[j % 2, pieces[p], :] = lax.dot_general(
            k_ref[pieces[p], :], q, NT_DIMS, preferred_element_type=F32)
        pm = jnp.max(s_ref[j % 2, pieces[p], :], axis=0, keepdims=True)
        return pm if mx is None else jnp.maximum(mx, pm)

    def exp_piece(j, p, m):
        e = jnp.exp2(s_ref[j % 2, pieces[p], :] - m)
        return e.astype(BF16), jnp.sum(e, axis=0, keepdims=True)

    def value_piece(p, e):
        return _dot(vt_ref[:, pieces[p]], e)

    m_next = None
    for p in range(len(pieces)):
        m_next = score_piece(0, p, m_next)
    for j in range(n_tiles):
        m, m_next = m_next, None
        acc = l = e_prev = None
        for p in range(len(pieces)):
            if j + 1 < n_tiles:
                m_next = score_piece(j + 1, p, m_next)
            if e_prev is not None:
                acc = add(acc, value_piece(p - 1, e_prev))
            e_prev, ls = exp_piece(j, p, m)
            l = add(l, ls)
        acc = add(acc, value_piece(len(pieces) - 1, e_prev))
        o_ref[j * Q_TILE:(j + 1) * Q_TILE, :] = (acc * (1.0 / l)).T.astype(BF16)


def _attn_call(q, ckv, kpe, wkv, wkv_t, batch, t, cache, tq):
    nq = t // tq
    past = cache[0].shape[1]
    tk = past + t
    assert t % KV_ROWS == 0 and past % LANES == 0 and tq % Q_TILE == 0
    return pl.pallas_call(
        functools.partial(_attn_kernel, t, past),
        grid=(batch, MLA_HEADS, nq),
        in_specs=[
            pl.BlockSpec((tq, HEAD_PAD), lambda b, h, i: (b * nq + i, h)),
            pl.BlockSpec((t, KV_RANK), lambda b, h, i: (b, 0)),
            pl.BlockSpec((t, ROPE_DIM), lambda b, h, i: (b, 0)),
            pl.BlockSpec((KV_RANK, NOPE_DIM), lambda b, h, i: (0, 2 * h)),
            pl.BlockSpec((V_DIM, KV_RANK), lambda b, h, i: (2 * h + 1, 0)),
            pl.BlockSpec((1, past, KV_RANK), lambda b, h, i: (b, 0, 0)),
            pl.BlockSpec((1, past, ROPE_DIM), lambda b, h, i: (b, 0, 0)),
        ],
        out_specs=pl.BlockSpec((tq, V_DIM), lambda b, h, i: (b * nq + i, h)),
        out_shape=jax.ShapeDtypeStruct((batch * t, MLA_HEADS * V_DIM), BF16),
        scratch_shapes=[
            pltpu.VMEM((tk, HEAD_PAD), BF16),
            pltpu.VMEM((V_DIM, tk), BF16),
            pltpu.VMEM((2, tk, Q_TILE), F32),
        ],
        compiler_params=_cparams(("arbitrary", "arbitrary", "arbitrary"), 56),
        name="attention",
    )(q, ckv, kpe, wkv, wkv_t, *cache)


def _attn_seq_kernel(q_ref, ckv_ref, kpe_ref, wkv_ref, o_ref):
    kv = _dot(ckv_ref[...].astype(BF16), wkv_ref[...])
    kpad = _rope_key_pad(kpe_ref[...])
    for hd in range(MLA_HEADS):
        c0 = hd * (NOPE_DIM + V_DIM)
        k = jnp.concatenate([kv[:, c0:c0 + NOPE_DIM].astype(BF16), kpad], axis=-1)
        s = lax.dot_general(q_ref[:, hd * HEAD_PAD:(hd + 1) * HEAD_PAD], k, NT_DIMS, preferred_element_type=F32)
        m = jnp.max(s, axis=-1, keepdims=True)
        e = jnp.exp2(s - m)
        l = jnp.sum(e, axis=-1, keepdims=True)
        o = _dot(e.astype(BF16), kv[:, c0 + NOPE_DIM:c0 + NOPE_DIM + V_DIM].astype(BF16)) * (1.0 / l)
        o_ref[:, hd * V_DIM:(hd + 1) * V_DIM] = o.astype(BF16)


def _attn_seq_call(q, ckv, kpe, wkv, batch, t):
    row = lambda b: (b, 0)
    return pl.pallas_call(
        _attn_seq_kernel,
        grid=(batch,),
        in_specs=[
            pl.BlockSpec((t, MLA_HEADS * HEAD_PAD), row),
            pl.BlockSpec((t, KV_RANK), row),
            pl.BlockSpec((t, ROPE_DIM), row),
            pl.BlockSpec(wkv.shape, lambda b: (0, 0)),
        ],
        out_specs=pl.BlockSpec((t, MLA_HEADS * V_DIM), row),
        out_shape=jax.ShapeDtypeStruct((batch * t, MLA_HEADS * V_DIM), BF16),
        compiler_params=_cparams(("arbitrary",), 32),
        name="attention_seq",
    )(q, ckv, kpe, wkv)


def _out_proj_kernel(n_in, x_ref, mods_ref, *rest):
    in_refs = rest[:n_in]
    w_ref, o_ref = rest[n_in:]
    acc = None
    k0 = 0
    for r in in_refs:
        k = r.shape[-1]
        part = _dot(r[...], w_ref[k0:k0 + k, :])
        acc = part if acc is None else acc + part
        k0 += k
    o_ref[...] = x_ref[...] + mods_ref[0][2:3, :] * acc


def _out_proj_call(x, mods, rows_per_cond, parts, w_o, tm=512):
    r = x.shape[0]
    row = lambda i: (i, 0)
    in_specs = [
        pl.BlockSpec((tm, D_MODEL), row),
        pl.BlockSpec((1, N_MODS, D_MODEL), lambda i: ((i * tm) // rows_per_cond, 0, 0)),
    ]
    in_specs += [pl.BlockSpec((tm, p.shape[-1]), row) for p in parts]
    in_specs += [pl.BlockSpec(w_o.shape, lambda i: (0, 0))]
    return pl.pallas_call(
        functools.partial(_out_proj_kernel, len(parts)),
        grid=(r // tm,),
        in_specs=in_specs,
        out_specs=pl.BlockSpec((tm, D_MODEL), row),
        out_shape=jax.ShapeDtypeStruct((r, D_MODEL), F32),
        compiler_params=_cparams(("arbitrary",), 48),
        name="out_proj",
    )(x, mods, *parts, w_o)


def _odd_in_kernel(tm, x_ref, mods_ref, g_ref, w_ref, lg_ref, lb_ref, ws_ref, bs_ref, o_ref):
    h = _modnorm(x_ref[...], g_ref[...], mods_ref[0], 0).astype(BF16)
    v = _gelu_tanh(_dot(h, w_ref[:, D_CMLP:]))
    v = _layernorm(v, lg_ref[...], lb_ref[...]).astype(BF16)
    u = _gelu_tanh(_dot(h, w_ref[:, :D_CMLP]))
    for ch in range(tm // CHUNK):
        rs = slice(ch * CHUNK, (ch + 1) * CHUNK)
        for g in range(CMLP_GROUPS):
            cs = slice(g * CMLP_GROUP_W, (g + 1) * CMLP_GROUP_W)
            mixed = _dot(ws_ref[g], v[rs, cs]) + bs_ref[:, cs]
            o_ref[rs, cs] = (u[rs, cs] * mixed).astype(BF16)


def _odd_in_call(x, mods, rows_per_cond, g_mix, w_in, ln_g, ln_b, w_s, b_s_full, tm=256):
    r = x.shape[0]
    const = lambda i: (0, 0)
    row = lambda i: (i, 0)
    return pl.pallas_call(
        functools.partial(_odd_in_kernel, tm),
        grid=(r // tm,),
        in_specs=[
            pl.BlockSpec((tm, D_MODEL), row),
            pl.BlockSpec((1, N_MODS, D_MODEL), lambda i: ((i * tm) // rows_per_cond, 0, 0)),
            pl.BlockSpec((1, D_MODEL), const),
            pl.BlockSpec(w_in.shape, const, pipeline_mode=pl.Buffered(1)),
            pl.BlockSpec((1, D_CMLP), const),
            pl.BlockSpec((1, D_CMLP), const),
            pl.BlockSpec(w_s.shape, lambda i: (0, 0, 0)),
            pl.BlockSpec((CHUNK, D_CMLP), const),
        ],
        out_specs=pl.BlockSpec((tm, D_CMLP), row),
        out_shape=jax.ShapeDtypeStruct((r, D_CMLP), BF16),
        compiler_params=_cparams(("arbitrary",), 56),
        name="odd_in",
    )(x, mods, g_mix, w_in, ln_g, ln_b, w_s, b_s_full)


def _ffn_kernel(final, x_ref, mods_ref, g_ref, wg_ref, wu_ref, wd_ref, *rest):
    if final:
        fg_ref, o_ref, h_ref, gs_ref, r_ref = rest
    else:
        o_ref, h_ref, gs_ref, r_ref = rest
    j = pl.program_id(1)
    tm = x_ref.shape[0]
    lane_tiles = [slice(c * LANES, (c + 1) * LANES) for c in range(D_MODEL // LANES)]

    @pl.when(j == 0)
    def _():
        gs_ref[0:1, :] = g_ref[...] * (1.0 + mods_ref[0][4:5, :])
        gs_ref[1:2, :] = mods_ref[0][3:4, :]
        _fill_rms_scale(tm, lambda rows: x_ref[rows, :], r_ref)

        def prologue(rows):
            r = r_ref[rows, :]
            for cs in lane_tiles:
                h_ref[rows, cs] = (x_ref[rows, cs] * r * gs_ref[0:1, cs] + gs_ref[1:2, cs]).astype(BF16)

        _row_loop(tm, NORM_ROWS, prologue, unroll=2)

    def down_proj():
        h = h_ref[...]
        hid = (_silu(_dot(h, wg_ref[...].astype(BF16))) * _dot(h, wu_ref[...].astype(BF16))).astype(BF16)
        return _dot(hid, wd_ref[...].astype(BF16))

    @pl.when(j == 0)
    def _():
        o_ref[...] = down_proj()

    @pl.when(j > 0)
    def _():
        o_ref[...] += down_proj()

    @pl.when(j == pl.num_programs(1) - 1)
    def _():
        o_ref[...] = x_ref[...] + mods_ref[0][5:6, :] * o_ref[...]
        if final:
            _fill_rms_scale(tm, lambda rows: o_ref[rows, :], r_ref)

            def final_norm(rows):
                r = r_ref[rows, :]
                for cs in lane_tiles:
                    o_ref[rows, cs] = o_ref[rows, cs] * r * fg_ref[:, cs]

            _row_loop(tm, NORM_ROWS, final_norm, unroll=2)


def _ffn_call(x, mods, rows_per_cond, g_ffn, layer, wg, wu, wd, final_g=None, tm=1024, tn=512):
    r = x.shape[0]
    final = final_g is not None
    const = lambda i, j: (0, 0)
    row = lambda i, j: (i, 0)
    in_specs = [
        pl.BlockSpec((tm, D_MODEL), row, pipeline_mode=pl.Buffered(1)),
        pl.BlockSpec((1, N_MODS, D_MODEL), lambda i, j: ((i * tm) // rows_per_cond, 0, 0)),
        pl.BlockSpec((1, D_MODEL), const),
        pl.BlockSpec((None, D_MODEL, tn), lambda i, j: (layer, 0, j)),
        pl.BlockSpec((None, D_MODEL, tn), lambda i, j: (layer, 0, j)),
        pl.BlockSpec((None, tn, D_MODEL), lambda i, j: (layer, j, 0)),
    ]
    args = [x, mods, g_ffn, wg, wu, wd]
    if final:
        in_specs.append(pl.BlockSpec((1, D_MODEL), const))
        args.append(final_g)
    return pl.pallas_call(
        functools.partial(_ffn_kernel, final),
        grid=(r // tm, D_FF // tn),
        in_specs=in_specs,
        out_specs=pl.BlockSpec((tm, D_MODEL), row),
        out_shape=jax.ShapeDtypeStruct((r, D_MODEL), F32),
        scratch_shapes=[
            pltpu.VMEM((tm, D_MODEL), BF16),
            pltpu.VMEM((SUBLANES, D_MODEL), F32),
            pltpu.VMEM((tm, LANES), F32),
        ],
        compiler_params=_cparams(("arbitrary", "arbitrary"), 60),
        name="ffn",
    )(*args)


def _rope_tables(n_tokens):
    n_freq = ROPE_DIM // 4
    lane = np.arange(ROPE_DIM)
    tok = np.arange(n_tokens)
    pos = np.where(lane[None, :] // (2 * n_freq) == 0, tok[:, None] // GRID_W, tok[:, None] % GRID_W).astype(np.float32)
    exponent = -np.arange(n_freq, dtype=np.float32) * np.float32(2.0) / np.float32(ROPE_DIM // 2)
    inv_freq = np.power(np.float32(ROPE_THETA), exponent).astype(np.float32)
    ang = pos * inv_freq[lane % n_freq][None, :]
    sign = np.where((lane // n_freq) % 2 == 0, -1.0, 1.0).astype(np.float32)
    return np.cos(ang).astype(np.float32), (np.sin(ang) * sign[None, :]).astype(np.float32)


def _pair_swap(w):
    perm = jnp.arange(ROPE_DIM) ^ (ROPE_DIM // 4)
    return w[..., perm]


def kernel(x_prompt, x_sample, cache_ckv, cache_kpe, c, c_ctx, mod_w, mod_b, norm_mix_g, norm_ffn_g, ffn_w_gate, ffn_w_up, ffn_w_down, ev_w_in, ev_conv_w, ev_conv_b, ev_conv_ln_g, ev_conv_ln_b, ev_q_norm_g, ev_w_qb, ev_kv_norm_g, ev_w_kvb, ev_w_o, od_w_in, od_ln_g, od_ln_b, od_w_s, od_b_s, od_w_o, final_norm_g):
    bp, tp, _ = x_prompt.shape
    bs, ts, _ = x_sample.shape
    xp = x_prompt.reshape(bp * tp, D_MODEL)
    xs = x_sample.reshape(bs * ts, D_MODEL)

    cond8 = jnp.concatenate([c_ctx[None, :], c, jnp.zeros((N_MODS - 1 - bs, D_MODEL), F32)], axis=0)
    mods = _mods_call(cond8, mod_w, mod_b)
    mods = mods[:, :1 + bs].reshape(DEPTH, 1 + bs, 6, D_MODEL)
    mods = jnp.pad(mods, ((0, 0), (0, 0), (0, N_MODS - 6), (0, 0)))

    cos_k, sin_k = _rope_tables(ts)
    zpad = np.zeros((ts, LANES - ROPE_DIM), np.float32)
    cos_q = np.concatenate([cos_k, zpad], axis=-1)
    sin_q = np.concatenate([sin_k, zpad], axis=-1)

    streams = [
        dict(x=xp, batch=bp, t=tp, rows=bp * tp, conds=slice(0, 1), rope=False),
        dict(x=xs, batch=bs, t=ts, rows=ts, conds=slice(1, 1 + bs), rope=True),
    ]
    state = {}
    for l in range(DEPTH):
        g_mix = norm_mix_g[l][None, :]
        g_ffn = norm_ffn_g[l][None, :]
        last = l == DEPTH - 1
        if l % 2 == 0:
            e = l // 2
            w_in = ev_w_in[e]
            w1 = jnp.concatenate([w_in, _pair_swap(w_in[:, O_KV:])], axis=1).astype(BF16)
            wq3 = ev_w_qb[e].reshape(Q_RANK, MLA_HEADS, QK_DIM)
            wq = jnp.pad(wq3, ((0, 0), (0, 0), (0, HEAD_PAD - QK_DIM))).reshape(Q_RANK, -1).astype(BF16)
            wqs = jnp.pad(_pair_swap(wq3[:, :, NOPE_DIM:]), ((0, 0), (0, 0), (0, LANES - ROPE_DIM)))
            wqs = wqs.reshape(Q_RANK, -1).astype(BF16)
            wkv = ev_w_kvb[e].astype(BF16)
            wkv_t = ev_w_kvb[e].T.astype(BF16)
            w_o = ev_w_o[e].astype(BF16)
            conv_w = ev_conv_w[e].reshape(CONV_WIDTH, D_CONV)
            for s in streams:
                m = mods[l, s["conds"]]
                rope_args = (wqs, cos_q, sin_q, cos_k, sin_k) if s["rope"] else None
                a, q, ckv, kpe = _even_in_call(s["x"], m, s["rows"], g_mix, w1, ev_q_norm_g[e][None, :],
                                               ev_kv_norm_g[e][None, :], wq, rope_args)
                a = _conv_call(a, s["batch"], s["t"], conv_w, ev_conv_b[e][None, :],
                               ev_conv_ln_g[e][None, :], ev_conv_ln_b[e][None, :], tt=min(s["t"], 256))
                if s["rope"]:
                    att = _attn_call(q, ckv, kpe, wkv, wkv_t, s["batch"], s["t"],
                                     (cache_ckv[:, e], cache_kpe[:, e]), tq=2048)
                else:
                    att = _attn_seq_call(q, ckv, kpe, wkv, s["batch"], s["t"])
                    state.setdefault("ckv", []).append(ckv.reshape(bp, tp, KV_RANK))
                    state.setdefault("kpe", []).append(kpe.reshape(bp, tp, ROPE_DIM))
                s["x"] = _out_proj_call(s["x"], m, s["rows"], [a, att], w_o)
        else:
            o = l // 2
            w_in = od_w_in[o].astype(BF16)
            w_o = od_w_o[o].astype(BF16)
            w_s = od_w_s[o].astype(BF16)
            b_s_full = jnp.repeat(od_b_s[o].T, CMLP_GROUP_W, axis=1)
            for s in streams:
                m = mods[l, s["conds"]]
                p = _odd_in_call(s["x"], m, s["rows"], g_mix, w_in, od_ln_g[o][None, :], od_ln_b[o][None, :],
                                 w_s, b_s_full)
                s["x"] = _out_proj_call(s["x"], m, s["rows"], [p], w_o)
        for s in streams:
            m = mods[l, s["conds"]]
            s["x"] = _ffn_call(s["x"], m, s["rows"], g_ffn, l, ffn_w_gate, ffn_w_up, ffn_w_down,
                               final_g=final_norm_g[None, :] if last else None)

    y_prompt = streams[0]["x"].reshape(bp, tp, D_MODEL)
    y_sample = streams[1]["x"].reshape(bs, ts, D_MODEL)
    state_ckv = jnp.stack(state["ckv"], axis=1)
    state_kpe = jnp.stack(state["kpe"], axis=1)
    return (y_prompt, y_sample, state_ckv, state_kpe)
```

```python
import functools

import jax
import jax.numpy as jnp
import numpy as np
from jax import lax
from jax.experimental import pallas as pl
from jax.experimental.pallas import tpu as pltpu

D_MODEL = 2048
DEPTH = 2
GRID_W = 64
D_CONV = D_MODEL // 2
CONV_WIDTH = 31
CONV_PAD = CONV_WIDTH // 2
MLA_HEADS = D_MODEL // 256
Q_RANK = D_MODEL // 4
KV_RANK = D_MODEL // 8
NOPE_DIM = 128
ROPE_DIM = 64
V_DIM = 128
QK_DIM = NOPE_DIM + ROPE_DIM
ROPE_THETA = 10000.0
D_CMLP = D_MODEL
CHUNK = 128
CMLP_GROUPS = 8
CMLP_GROUP_W = D_CMLP // CMLP_GROUPS
D_FF = 256 * (-(-8 * D_MODEL // (3 * 256)))
ATTN_SCALE = QK_DIM ** -0.5
EXP2_SCALE = ATTN_SCALE * 1.4426950408889634
EPS = 1e-6

LANES = 128
SUBLANES = 8
HEAD_PAD = 2 * LANES
CONV_HALO = 2 * SUBLANES
N_MODS = 8
MIB = 1024 * 1024

BF16 = jnp.bfloat16
F32 = jnp.float32


def _cparams(semantics, vmem_mib):
    return pltpu.CompilerParams(dimension_semantics=semantics, vmem_limit_bytes=vmem_mib * MIB)


def _silu(x):
    return x * (1.0 / (1.0 + jnp.exp(-x)))


def _gelu_tanh(x):
    cdf = 0.5 * (1.0 + jnp.tanh(0.7978845608028654 * (x + 0.044715 * (x * x * x))))
    return x * cdf


def _rms(x, g):
    return (x * lax.rsqrt(jnp.mean(x * x, axis=-1, keepdims=True) + EPS)) * g


def _layernorm(x, g, b):
    mu = jnp.mean(x, axis=-1, keepdims=True)
    xc = x - mu
    var = jnp.mean(xc * xc, axis=-1, keepdims=True)
    return xc * lax.rsqrt(var + EPS) * g + b


def _modnorm(x, g, mods, row):
    return _rms(x, g) * (1.0 + mods[row + 1:row + 2, :]) + mods[row:row + 1, :]


def _dot(a, b):
    return jnp.dot(a, b, preferred_element_type=F32)


NORM_ROWS = 2 * SUBLANES


def _row_loop(n_rows, block, body, unroll=1):
    def step(i, carry):
        body(pl.ds(pl.multiple_of(i * block, block), block))
        return carry

    lax.fori_loop(0, n_rows // block, step, 0, unroll=unroll)


def _fill_rms_scale(n_rows, read, r_ref):
    def body(rows):
        x = read(rows)
        ms = jnp.mean(x * x, axis=-1, keepdims=True)
        r_ref[rows, :] = jnp.broadcast_to(lax.rsqrt(ms + EPS), (SUBLANES, LANES))

    _row_loop(n_rows, SUBLANES, body, unroll=32)


def _mods_kernel(cond_ref, w_ref, b_ref, o_ref):
    s = _silu(cond_ref[...]).astype(BF16)
    o_ref[0] = _dot(s, w_ref[0].astype(BF16)) + b_ref[0]


def _mods_call(cond8, mod_w, mod_b):
    tn = 1024
    n = mod_w.shape[-1]
    return pl.pallas_call(
        _mods_kernel,
        grid=(DEPTH, n // tn),
        in_specs=[
            pl.BlockSpec((N_MODS, D_MODEL), lambda l, j: (0, 0)),
            pl.BlockSpec((1, D_MODEL, tn), lambda l, j: (l, 0, j)),
            pl.BlockSpec((1, 1, tn), lambda l, j: (l, 0, j)),
        ],
        out_specs=pl.BlockSpec((1, N_MODS, tn), lambda l, j: (l, 0, j)),
        out_shape=jax.ShapeDtypeStruct((DEPTH, N_MODS, n), F32),
        compiler_params=_cparams(("arbitrary", "arbitrary"), 32),
        name="mods",
    )(cond8, mod_w, mod_b.reshape(DEPTH, 1, n))


O_GLU = 2 * D_CONV
O_Q = O_GLU + Q_RANK
O_KV = O_Q + KV_RANK
O_END = O_KV + 2 * ROPE_DIM


def _even_in_kernel(rope, x_ref, mods_ref, g_ref, w_ref, gq_ref, gkv_ref, wq_ref, *rest):
    if rope:
        wqs_ref, cq_ref, sq_ref, ck_ref, sk_ref, a_ref, q_ref, ckv_ref, kpe_ref = rest
    else:
        a_ref, q_ref, ckv_ref, kpe_ref = rest
    h = _modnorm(x_ref[...], g_ref[...], mods_ref[0], 0).astype(BF16)
    lin = _dot(h, w_ref[:, :D_CONV])
    gate = _dot(h, w_ref[:, D_CONV:O_GLU])
    a_ref[...] = lin * (1.0 / (1.0 + jnp.exp(-gate)))
    qn = _rms(_dot(h, w_ref[:, O_GLU:O_Q]), gq_ref[...]).astype(BF16)
    ckv_ref[...] = _rms(_dot(h, w_ref[:, O_Q:O_KV]), gkv_ref[...])
    kp = _dot(h, w_ref[:, O_KV:O_END])
    q = _dot(qn, wq_ref[...]) * EXP2_SCALE
    if rope:
        kpe_ref[...] = kp[:, :ROPE_DIM] * ck_ref[...] + kp[:, ROPE_DIM:] * sk_ref[...]
        qs = _dot(qn, wqs_ref[...]) * EXP2_SCALE
        cq = cq_ref[...]
        sq = sq_ref[...]
        for hd in range(MLA_HEADS):
            c0 = hd * HEAD_PAD
            q_ref[:, c0:c0 + LANES] = q[:, c0:c0 + LANES].astype(BF16)
            q_ref[:, c0 + LANES:c0 + HEAD_PAD] = (
                q[:, c0 + LANES:c0 + HEAD_PAD] * cq + qs[:, hd * LANES:(hd + 1) * LANES] * sq).astype(BF16)
    else:
        kpe_ref[...] = kp[:, :ROPE_DIM]
        q_ref[...] = q.astype(BF16)


def _even_in_call(x, mods, rows_per_cond, g_mix, w1, g_qa, g_kva, wq, rope_args, tm=256):
    r = x.shape[0]
    rope = rope_args is not None
    const = lambda i: (0, 0)
    row = lambda i: (i, 0)
    in_specs = [
        pl.BlockSpec((tm, D_MODEL), row),
        pl.BlockSpec((1, N_MODS, D_MODEL), lambda i: ((i * tm) // rows_per_cond, 0, 0)),
        pl.BlockSpec((1, D_MODEL), const),
        pl.BlockSpec(w1.shape, const),
        pl.BlockSpec((1, Q_RANK), const),
        pl.BlockSpec((1, KV_RANK), const),
        pl.BlockSpec(wq.shape, const),
    ]
    args = [x, mods, g_mix, w1, g_qa, g_kva, wq]
    if rope:
        wqs, cq, sq, ck, sk = rope_args
        t_blocks = cq.shape[0] // tm
        pos = lambda i: (i % t_blocks, 0)
        in_specs += [
            pl.BlockSpec(wqs.shape, const),
            pl.BlockSpec((tm, LANES), pos),
            pl.BlockSpec((tm, LANES), pos),
            pl.BlockSpec((tm, ROPE_DIM), pos),
            pl.BlockSpec((tm, ROPE_DIM), pos),
        ]
        args += [wqs, cq, sq, ck, sk]
    return pl.pallas_call(
        functools.partial(_even_in_kernel, rope),
        grid=(r // tm,),
        in_specs=in_specs,
        out_specs=[
            pl.BlockSpec((tm, D_CONV), row),
            pl.BlockSpec((tm, MLA_HEADS * HEAD_PAD), row),
            pl.BlockSpec((tm, KV_RANK), row),
            pl.BlockSpec((tm, ROPE_DIM), row),
        ],
        out_shape=[
            jax.ShapeDtypeStruct((r, D_CONV), F32),
            jax.ShapeDtypeStruct((r, MLA_HEADS * HEAD_PAD), BF16),
            jax.ShapeDtypeStruct((r, KV_RANK), F32),
            jax.ShapeDtypeStruct((r, ROPE_DIM), F32),
        ],
        compiler_params=_cparams(("arbitrary",), 56),
        name="even_in",
    )(*args)


CONV_ROWS = 64
SHIFT_ROWS = 40


def _conv_kernel(tt, n_t, cur_ref, prev_ref, next_ref, w_ref, b_ref, g_ref, beta_ref, o_ref, buf_ref, sh_ref, y_ref):
    i = pl.program_id(1)
    keep_prev = (i > 0).astype(F32)
    keep_next = (i < n_t - 1).astype(F32)
    buf_ref[0:CONV_HALO, :] = prev_ref[0] * keep_prev
    buf_ref[CONV_HALO:CONV_HALO + tt, :] = cur_ref[0]
    buf_ref[CONV_HALO + tt:, :] = next_ref[0] * keep_next
    lane_tiles = [slice(c * LANES, (c + 1) * LANES) for c in range(D_CONV // LANES)]

    def shift(rows):
        for cs in lane_tiles:
            win = buf_ref[pl.ds(rows.start, SHIFT_ROWS + SUBLANES), cs]
            for r in range(1, SUBLANES):
                sh_ref[r - 1, rows, cs] = win[r:r + SHIFT_ROWS, :]

    _row_loop(tt + 2 * CONV_HALO - SUBLANES, SHIFT_ROWS, shift)

    first = CONV_HALO - CONV_PAD
    n_win = CONV_ROWS + (first + CONV_WIDTH - 1) // SUBLANES * SUBLANES

    def taps(rows):
        win_rows = pl.ds(rows.start, n_win)
        for cs in lane_tiles:
            acc = b_ref[:, cs]
            for r in range(SUBLANES):
                win = buf_ref[win_rows, cs] if r == 0 else sh_ref[r - 1, win_rows, cs]
                for q in range(n_win // SUBLANES):
                    k = q * SUBLANES + r - first
                    if 0 <= k < CONV_WIDTH:
                        acc = acc + win[q * SUBLANES:q * SUBLANES + CONV_ROWS, :] * w_ref[k:k + 1, cs]
            y_ref[rows, cs] = acc

    _row_loop(tt, CONV_ROWS, taps)
    o_ref[0] = _silu(_layernorm(y_ref[...], g_ref[...], beta_ref[...])).astype(BF16)


def _conv_call(a, batch, t, conv_w, conv_b, ln_g, ln_b, tt):
    n_t = t // tt
    assert tt % CONV_ROWS == 0 and (tt + 2 * CONV_HALO - SUBLANES) % SHIFT_ROWS == 0
    a3 = a.reshape(batch, t, D_CONV)
    hb = tt // CONV_HALO
    n_hb = t // CONV_HALO
    const = lambda b, i: (0, 0)
    out = pl.pallas_call(
        functools.partial(_conv_kernel, tt, n_t),
        grid=(batch, n_t),
        in_specs=[
            pl.BlockSpec((1, tt, D_CONV), lambda b, i: (b, i, 0)),
            pl.BlockSpec((1, CONV_HALO, D_CONV), lambda b, i: (b, jnp.maximum(i * hb - 1, 0), 0)),
            pl.BlockSpec((1, CONV_HALO, D_CONV), lambda b, i: (b, jnp.minimum((i + 1) * hb, n_hb - 1), 0)),
            pl.BlockSpec((CONV_WIDTH, D_CONV), const),
            pl.BlockSpec((1, D_CONV), const),
            pl.BlockSpec((1, D_CONV), const),
            pl.BlockSpec((1, D_CONV), const),
        ],
        out_specs=pl.BlockSpec((1, tt, D_CONV), lambda b, i: (b, i, 0)),
        out_shape=jax.ShapeDtypeStruct((batch, t, D_CONV), BF16),
        scratch_shapes=[
            pltpu.VMEM((tt + 2 * CONV_HALO, D_CONV), F32),
            pltpu.VMEM((SUBLANES - 1, tt + 2 * CONV_HALO - SUBLANES, D_CONV), F32),
            pltpu.VMEM((tt, D_CONV), F32),
        ],
        compiler_params=_cparams(("arbitrary", "arbitrary"), 32),
        name="conv",
    )(a3, a3, a3, conv_w, conv_b, ln_g, ln_b)
    return out.reshape(batch * t, D_CONV)


KV_ROWS = 1024
Q_TILE = 256
KEY_PIECE = 1024
NT_DIMS = (((1,), (1,)), ((), ()))


def _rope_key_pad(kpe):
    return jnp.concatenate([kpe, jnp.zeros((kpe.shape[0], HEAD_PAD - QK_DIM), F32)], axis=-1).astype(BF16)


def _fill_kv(k_ref, vt_ref, row0, ckv, kpe, wk, wvt):
    c = ckv.astype(BF16)
    n = c.shape[0]
    k_ref[row0:row0 + n, 0:NOPE_DIM] = _dot(c, wk).astype(BF16)
    k_ref[row0:row0 + n, NOPE_DIM:HEAD_PAD] = _rope_key_pad(kpe)
    vt_ref[:, row0:row0 + n] = lax.dot_general(wvt, c, NT_DIMS, preferred_element_type=F32).astype(BF16)


def _attn_kernel(t, past, q_ref, ckv_ref, kpe_ref, wk_ref, wvt_ref, cckv_ref, ckpe_ref, o_ref, k_ref, vt_ref, s_ref):
    @pl.when(pl.program_id(2) == 0)
    def _():
        wk = wk_ref[...]
        wvt = wvt_ref[...]
        _fill_kv(k_ref, vt_ref, 0, cckv_ref[0], ckpe_ref[0], wk, wvt)
        for r0 in range(0, t, KV_ROWS):
            _fill_kv(k_ref, vt_ref, past + r0, ckv_ref[r0:r0 + KV_ROWS, :], kpe_ref[r0:r0 + KV_ROWS, :], wk, wvt)

    n_tiles = q_ref.shape[0] // Q_TILE
    tk = k_ref.shape[0]
    pieces = [slice(k0, min(k0 + KEY_PIECE, tk)) for k0 in range(0, tk, KEY_PIECE)]

    def add(a, b):
        return b if a is None else a + b

    def score_piece(j, p, mx):
        q = q_ref[j * Q_TILE:(j + 1) * Q_TILE, :]
        s_ref[j % 2, pieces[p], :] = lax.dot_general(
            k_ref[pieces[p], :], q, NT_DIMS, preferred_element_type=F32)
        pm = jnp.max(s_ref[j % 2, pieces[p], :], axis=0, keepdims=True)
        return pm if mx is None else jnp.maximum(mx, pm)

    def exp_piece(j, p, m):
        e = jnp.exp2(s_ref[j % 2, pieces[p], :] - m)
        return e.astype(BF16), jnp.sum(e, axis=0, keepdims=True)

    def value_piece(p, e):
        return _dot(vt_ref[:, pieces[p]], e)

    m_next = None
    for p in range(len(pieces)):
        m_next = score_piece(0, p, m_next)
    for j in range(n_tiles):
        m, m_next = m_next, None
        acc = l = e_prev = None
        for p in range(len(pieces)):
            if j + 1 < n_tiles:
                m_next = score_piece(j + 1, p, m_next)
            if e_prev is not None:
                acc = add(acc, value_piece(p - 1, e_prev))
            e_prev, ls = exp_piece(j, p, m)
            l = add(l, ls)
        acc = add(acc, value_piece(len(pieces) - 1, e_prev))
        o_ref[j * Q_TILE:(j + 1) * Q_TILE, :] = (acc * (1.0 / l)).T.astype(BF16)


def _attn_call(q, ckv, kpe, wkv, wkv_t, batch, t, cache, tq):
    nq = t // tq
    past = cache[0].shape[1]
    tk = past + t
    assert t % KV_ROWS == 0 and past % LANES == 0 and tq % Q_TILE == 0
    return pl.pallas_call(
        functools.partial(_attn_kernel, t, past),
        grid=(batch, MLA_HEADS, nq),
        in_specs=[
            pl.BlockSpec((tq, HEAD_PAD), lambda b, h, i: (b * nq + i, h)),
            pl.BlockSpec((t, KV_RANK), lambda b, h, i: (b, 0)),
            pl.BlockSpec((t, ROPE_DIM), lambda b, h, i: (b, 0)),
            pl.BlockSpec((KV_RANK, NOPE_DIM), lambda b, h, i: (0, 2 * h)),
            pl.BlockSpec((V_DIM, KV_RANK), lambda b, h, i: (2 * h + 1, 0)),
            pl.BlockSpec((1, past, KV_RANK), lambda b, h, i: (b, 0, 0)),
            pl.BlockSpec((1, past, ROPE_DIM), lambda b, h, i: (b, 0, 0)),
        ],
        out_specs=pl.BlockSpec((tq, V_DIM), lambda b, h, i: (b * nq + i, h)),
        out_shape=jax.ShapeDtypeStruct((batch * t, MLA_HEADS * V_DIM), BF16),
        scratch_shapes=[
            pltpu.VMEM((tk, HEAD_PAD), BF16),
            pltpu.VMEM((V_DIM, tk), BF16),
            pltpu.VMEM((2, tk, Q_TILE), F32),
        ],
        compiler_params=_cparams(("arbitrary", "arbitrary", "arbitrary"), 56),
        name="attention",
    )(q, ckv, kpe, wkv, wkv_t, *cache)


def _attn_seq_kernel(q_ref, ckv_ref, kpe_ref, wkv_ref, o_ref):
    kv = _dot(ckv_ref[...].astype(BF16), wkv_ref[...])
    kpad = _rope_key_pad(kpe_ref[...])
    for hd in range(MLA_HEADS):
        c0 = hd * (NOPE_DIM + V_DIM)
        k = jnp.concatenate([kv[:, c0:c0 + NOPE_DIM].astype(BF16), kpad], axis=-1)
        s = lax.dot_general(q_ref[:, hd * HEAD_PAD:(hd + 1) * HEAD_PAD], k, NT_DIMS, preferred_element_type=F32)
        m = jnp.max(s, axis=-1, keepdims=True)
        e = jnp.exp2(s - m)
        l = jnp.sum(e, axis=-1, keepdims=True)
        o = _dot(e.astype(BF16), kv[:, c0 + NOPE_DIM:c0 + NOPE_DIM + V_DIM].astype(BF16)) * (1.0 / l)
        o_ref[:, hd * V_DIM:(hd + 1) * V_DIM] = o.astype(BF16)


def _attn_seq_call(q, ckv, kpe, wkv, batch, t):
    row = lambda b: (b, 0)
    return pl.pallas_call(
        _attn_seq_kernel,
        grid=(batch,),
        in_specs=[
            pl.BlockSpec((t, MLA_HEADS * HEAD_PAD), row),
            pl.BlockSpec((t, KV_RANK), row),
            pl.BlockSpec((t, ROPE_DIM), row),
            pl.BlockSpec(wkv.shape, lambda b: (0, 0)),
        ],
        out_specs=pl.BlockSpec((t, MLA_HEADS * V_DIM), row),
        out_shape=jax.ShapeDtypeStruct((batch * t, MLA_HEADS * V_DIM), BF16),
        compiler_params=_cparams(("arbitrary",), 32),
        name="attention_seq",
    )(q, ckv, kpe, wkv)


def _out_proj_kernel(n_in, x_ref, mods_ref, *rest):
    in_refs = rest[:n_in]
    w_ref, o_ref = rest[n_in:]
    acc = None
    k0 = 0
    for r in in_refs:
        k = r.shape[-1]
        part = _dot(r[...], w_ref[k0:k0 + k, :])
        acc = part if acc is None else acc + part
        k0 += k
    o_ref[...] = x_ref[...] + mods_ref[0][2:3, :] * acc


def _out_proj_call(x, mods, rows_per_cond, parts, w_o, tm=512):
    r = x.shape[0]
    row = lambda i: (i, 0)
    in_specs = [
        pl.BlockSpec((tm, D_MODEL), row),
        pl.BlockSpec((1, N_MODS, D_MODEL), lambda i: ((i * tm) // rows_per_cond, 0, 0)),
    ]
    in_specs += [pl.BlockSpec((tm, p.shape[-1]), row) for p in parts]
    in_specs += [pl.BlockSpec(w_o.shape, lambda i: (0, 0))]
    return pl.pallas_call(
        functools.partial(_out_proj_kernel, len(parts)),
        grid=(r // tm,),
        in_specs=in_specs,
        out_specs=pl.BlockSpec((tm, D_MODEL), row),
        out_shape=jax.ShapeDtypeStruct((r, D_MODEL), F32),
        compiler_params=_cparams(("arbitrary",), 48),
        name="out_proj",
    )(x, mods, *parts, w_o)


def _odd_in_kernel(tm, x_ref, mods_ref, g_ref, w_ref, lg_ref, lb_ref, ws_ref, bs_ref, o_ref):
    h = _modnorm(x_ref[...], g_ref[...], mods_ref[0], 0).astype(BF16)
    v = _gelu_tanh(_dot(h, w_ref[:, D_CMLP:]))
    v = _layernorm(v, lg_ref[...], lb_ref[...]).astype(BF16)
    u = _gelu_tanh(_dot(h, w_ref[:, :D_CMLP]))
    for ch in range(tm // CHUNK):
        rs = slice(ch * CHUNK, (ch + 1) * CHUNK)
        for g in range(CMLP_GROUPS):
            cs = slice(g * CMLP_GROUP_W, (g + 1) * CMLP_GROUP_W)
            mixed = _dot(ws_ref[g], v[rs, cs]) + bs_ref[:, cs]
            o_ref[rs, cs] = (u[rs, cs] * mixed).astype(BF16)


def _odd_in_call(x, mods, rows_per_cond, g_mix, w_in, ln_g, ln_b, w_s, b_s_full, tm=256):
    r = x.shape[0]
    const = lambda i: (0, 0)
    row = lambda i: (i, 0)
    return pl.pallas_call(
        functools.partial(_odd_in_kernel, tm),
        grid=(r // tm,),
        in_specs=[
            pl.BlockSpec((tm, D_MODEL), row),
            pl.BlockSpec((1, N_MODS, D_MODEL), lambda i: ((i * tm) // rows_per_cond, 0, 0)),
            pl.BlockSpec((1, D_MODEL), const),
            pl.BlockSpec(w_in.shape, const, pipeline_mode=pl.Buffered(1)),
            pl.BlockSpec((1, D_CMLP), const),
            pl.BlockSpec((1, D_CMLP), const),
            pl.BlockSpec(w_s.shape, lambda i: (0, 0, 0)),
            pl.BlockSpec((CHUNK, D_CMLP), const),
        ],
        out_specs=pl.BlockSpec((tm, D_CMLP), row),
        out_shape=jax.ShapeDtypeStruct((r, D_CMLP), BF16),
        compiler_params=_cparams(("arbitrary",), 56),
        name="odd_in",
    )(x, mods, g_mix, w_in, ln_g, ln_b, w_s, b_s_full)


def _ffn_kernel(final, x_ref, mods_ref, g_ref, wg_ref, wu_ref, wd_ref, *rest):
    if final:
        fg_ref, o_ref, h_ref, gs_ref, r_ref = rest
    else:
        o_ref, h_ref, gs_ref, r_ref = rest
    j = pl.program_id(1)
    tm = x_ref.shape[0]
    lane_tiles = [slice(c * LANES, (c + 1) * LANES) for c in range(D_MODEL // LANES)]

    @pl.when(j == 0)
    def _():
        gs_ref[0:1, :] = g_ref[...] * (1.0 + mods_ref[0][4:5, :])
        gs_ref[1:2, :] = mods_ref[0][3:4, :]
        gs_ref[2:3, :] = mods_ref[0][5:6, :]
        _fill_rms_scale(tm, lambda rows: x_ref[rows, :], r_ref)

        def prologue(rows):
            r = r_ref[rows, :]
            for cs in lane_tiles:
                h_ref[rows, cs] = (x_ref[rows, cs] * r * gs_ref[0:1, cs] + gs_ref[1:2, cs]).astype(BF16)

        _row_loop(tm, NORM_ROWS, prologue, unroll=2)

    def gated_down_proj():
        h = h_ref[...]
        hid = (_silu(_dot(h, wg_ref[...].astype(BF16))) * _dot(h, wu_ref[...].astype(BF16))).astype(BF16)
        return gs_ref[2:3, :] * _dot(hid, wd_ref[...].astype(BF16))

    @pl.when(j == 0)
    def _():
        o_ref[...] = x_ref[...] + gated_down_proj()

    @pl.when(j > 0)
    def _():
        o_ref[...] += gated_down_proj()

    if final:
        @pl.when(j == pl.num_programs(1) - 1)
        def _():
            _fill_rms_scale(tm, lambda rows: o_ref[rows, :], r_ref)

            def final_norm(rows):
                r = r_ref[rows, :]
                for cs in lane_tiles:
                    o_ref[rows, cs] = o_ref[rows, cs] * r * fg_ref[:, cs]

            _row_loop(tm, NORM_ROWS, final_norm, unroll=2)


def _ffn_call(x, mods, rows_per_cond, g_ffn, layer, wg, wu, wd, final_g=None, tm=1024, tn=256):
    r = x.shape[0]
    final = final_g is not None
    const = lambda i, j: (0, 0)
    row = lambda i, j: (i, 0)
    in_specs = [
        pl.BlockSpec((tm, D_MODEL), row),
        pl.BlockSpec((1, N_MODS, D_MODEL), lambda i, j: ((i * tm) // rows_per_cond, 0, 0)),
        pl.BlockSpec((1, D_MODEL), const),
        pl.BlockSpec((None, D_MODEL, tn), lambda i, j: (layer, 0, j)),
        pl.BlockSpec((None, D_MODEL, tn), lambda i, j: (layer, 0, j)),
        pl.BlockSpec((None, tn, D_MODEL), lambda i, j: (layer, j, 0)),
    ]
    args = [x, mods, g_ffn, wg, wu, wd]
    if final:
        in_specs.append(pl.BlockSpec((1, D_MODEL), const))
        args.append(final_g)
    return pl.pallas_call(
        functools.partial(_ffn_kernel, final),
        grid=(r // tm, D_FF // tn),
        in_specs=in_specs,
        out_specs=pl.BlockSpec((tm, D_MODEL), row),
        out_shape=jax.ShapeDtypeStruct((r, D_MODEL), F32),
        scratch_shapes=[
            pltpu.VMEM((tm, D_MODEL), BF16),
            pltpu.VMEM((SUBLANES, D_MODEL), F32),
            pltpu.VMEM((tm, LANES), F32),
        ],
        compiler_params=_cparams(("arbitrary", "arbitrary"), 60),
        name="ffn",
    )(*args)


def _rope_tables(n_tokens):
    n_freq = ROPE_DIM // 4
    lane = np.arange(ROPE_DIM)
    tok = np.arange(n_tokens)
    pos = np.where(lane[None, :] // (2 * n_freq) == 0, tok[:, None] // GRID_W, tok[:, None] % GRID_W).astype(np.float32)
    exponent = -np.arange(n_freq, dtype=np.float32) * np.float32(2.0) / np.float32(ROPE_DIM // 2)
    inv_freq = np.power(np.float32(ROPE_THETA), exponent).astype(np.float32)
    ang = pos * inv_freq[lane % n_freq][None, :]
    sign = np.where((lane // n_freq) % 2 == 0, -1.0, 1.0).astype(np.float32)
    return np.cos(ang).astype(np.float32), (np.sin(ang) * sign[None, :]).astype(np.float32)


def _pair_swap(w):
    perm = jnp.arange(ROPE_DIM) ^ (ROPE_DIM // 4)
    return w[..., perm]


def kernel(x_prompt, x_sample, cache_ckv, cache_kpe, c, c_ctx, mod_w, mod_b, norm_mix_g, norm_ffn_g, ffn_w_gate, ffn_w_up, ffn_w_down, ev_w_in, ev_conv_w, ev_conv_b, ev_conv_ln_g, ev_conv_ln_b, ev_q_norm_g, ev_w_qb, ev_kv_norm_g, ev_w_kvb, ev_w_o, od_w_in, od_ln_g, od_ln_b, od_w_s, od_b_s, od_w_o, final_norm_g):
    bp, tp, _ = x_prompt.shape
    bs, ts, _ = x_sample.shape
    xp = x_prompt.reshape(bp * tp, D_MODEL)
    xs = x_sample.reshape(bs * ts, D_MODEL)

    cond8 = jnp.concatenate([c_ctx[None, :], c, jnp.zeros((N_MODS - 1 - bs, D_MODEL), F32)], axis=0)
    mods = _mods_call(cond8, mod_w, mod_b)
    mods = mods[:, :1 + bs].reshape(DEPTH, 1 + bs, 6, D_MODEL)
    mods = jnp.pad(mods, ((0, 0), (0, 0), (0, N_MODS - 6), (0, 0)))

    cos_k, sin_k = _rope_tables(ts)
    zpad = np.zeros((ts, LANES - ROPE_DIM), np.float32)
    cos_q = np.concatenate([cos_k, zpad], axis=-1)
    sin_q = np.concatenate([sin_k, zpad], axis=-1)

    streams = [
        dict(x=xp, batch=bp, t=tp, rows=bp * tp, conds=slice(0, 1), rope=False),
        dict(x=xs, batch=bs, t=ts, rows=ts, conds=slice(1, 1 + bs), rope=True),
    ]
    state = {}
    for l in range(DEPTH):
        g_mix = norm_mix_g[l][None, :]
        g_ffn = norm_ffn_g[l][None, :]
        last = l == DEPTH - 1
        if l % 2 == 0:
            e = l // 2
            w_in = ev_w_in[e]
            w1 = jnp.concatenate([w_in, _pair_swap(w_in[:, O_KV:])], axis=1).astype(BF16)
            wq3 = ev_w_qb[e].reshape(Q_RANK, MLA_HEADS, QK_DIM)
            wq = jnp.pad(wq3, ((0, 0), (0, 0), (0, HEAD_PAD - QK_DIM))).reshape(Q_RANK, -1).astype(BF16)
            wqs = jnp.pad(_pair_swap(wq3[:, :, NOPE_DIM:]), ((0, 0), (0, 0), (0, LANES - ROPE_DIM)))
            wqs = wqs.reshape(Q_RANK, -1).astype(BF16)
            wkv = ev_w_kvb[e].astype(BF16)
            wkv_t = ev_w_kvb[e].T.astype(BF16)
            w_o = ev_w_o[e].astype(BF16)
            conv_w = ev_conv_w[e].reshape(CONV_WIDTH, D_CONV)
            for s in streams:
                m = mods[l, s["conds"]]
                rope_args = (wqs, cos_q, sin_q, cos_k, sin_k) if s["rope"] else None
                a, q, ckv, kpe = _even_in_call(s["x"], m, s["rows"], g_mix, w1, ev_q_norm_g[e][None, :],
                                               ev_kv_norm_g[e][None, :], wq, rope_args)
                a = _conv_call(a, s["batch"], s["t"], conv_w, ev_conv_b[e][None, :],
                               ev_conv_ln_g[e][None, :], ev_conv_ln_b[e][None, :], tt=min(s["t"], 256))
                if s["rope"]:
                    att = _attn_call(q, ckv, kpe, wkv, wkv_t, s["batch"], s["t"],
                                     (cache_ckv[:, e], cache_kpe[:, e]), tq=2048)
                else:
                    att = _attn_seq_call(q, ckv, kpe, wkv, s["batch"], s["t"])
                    state.setdefault("ckv", []).append(ckv.reshape(bp, tp, KV_RANK))
                    state.setdefault("kpe", []).append(kpe.reshape(bp, tp, ROPE_DIM))
                s["x"] = _out_proj_call(s["x"], m, s["rows"], [a, att], w_o)
        else:
            o = l // 2
            w_in = od_w_in[o].astype(BF16)
            w_o = od_w_o[o].astype(BF16)
            w_s = od_w_s[o].astype(BF16)
            b_s_full = jnp.repeat(od_b_s[o].T, CMLP_GROUP_W, axis=1)
            for s in streams:
                m = mods[l, s["conds"]]
                p = _odd_in_call(s["x"], m, s["rows"], g_mix, w_in, od_ln_g[o][None, :], od_ln_b[o][None, :],
                                 w_s, b_s_full)
                s["x"] = _out_proj_call(s["x"], m, s["rows"], [p], w_o)
        for s in streams:
            m = mods[l, s["conds"]]
            s["x"] = _ffn_call(s["x"], m, s["rows"], g_ffn, l, ffn_w_gate, ffn_w_up, ffn_w_down,
                               final_g=final_norm_g[None, :] if last else None)

    y_prompt = streams[0]["x"].reshape(bp, tp, D_MODEL)
    y_sample = streams[1]["x"].reshape(bs, ts, D_MODEL)
    state_ckv = jnp.stack(state["ckv"], axis=1)
    state_kpe = jnp.stack(state["kpe"], axis=1)
    return (y_prompt, y_sample, state_ckv, state_kpe)
```

```python
import functools

import jax
import jax.numpy as jnp
import numpy as np
from jax import lax
from jax.experimental import pallas as pl
from jax.experimental.pallas import tpu as pltpu

D_MODEL = 2048
DEPTH = 2
GRID_W = 64
D_CONV = D_MODEL // 2
CONV_WIDTH = 31
CONV_PAD = CONV_WIDTH // 2
MLA_HEADS = D_MODEL // 256
Q_RANK = D_MODEL // 4
KV_RANK = D_MODEL // 8
NOPE_DIM = 128
ROPE_DIM = 64
V_DIM = 128
QK_DIM = NOPE_DIM + ROPE_DIM
ROPE_THETA = 10000.0
D_CMLP = D_MODEL
CHUNK = 128
CMLP_GROUPS = 8
CMLP_GROUP_W = D_CMLP // CMLP_GROUPS
D_FF = 256 * (-(-8 * D_MODEL // (3 * 256)))
ATTN_SCALE = QK_DIM ** -0.5
EXP2_SCALE = ATTN_SCALE * 1.4426950408889634
EPS = 1e-6

LANES = 128
SUBLANES = 8
HEAD_PAD = 2 * LANES
CONV_HALO = 2 * SUBLANES
N_MODS = 8
MIB = 1024 * 1024

BF16 = jnp.bfloat16
F32 = jnp.float32


def _cparams(semantics, vmem_mib):
    return pltpu.CompilerParams(dimension_semantics=semantics, vmem_limit_bytes=vmem_mib * MIB)


def _silu(x):
    return x * (1.0 / (1.0 + jnp.exp(-x)))


def _gelu_tanh(x):
    cdf = 0.5 * (1.0 + jnp.tanh(0.7978845608028654 * (x + 0.044715 * (x * x * x))))
    return x * cdf


def _rms(x, g):
    return (x * lax.rsqrt(jnp.mean(x * x, axis=-1, keepdims=True) + EPS)) * g


def _layernorm(x, g, b):
    mu = jnp.mean(x, axis=-1, keepdims=True)
    xc = x - mu
    var = jnp.mean(xc * xc, axis=-1, keepdims=True)
    return xc * lax.rsqrt(var + EPS) * g + b


def _modnorm(x, g, mods, row):
    return _rms(x, g) * (1.0 + mods[row + 1:row + 2, :]) + mods[row:row + 1, :]


def _dot(a, b):
    return jnp.dot(a, b, preferred_element_type=F32)


NORM_ROWS = 2 * SUBLANES


def _row_loop(n_rows, block, body, unroll=1):
    def step(i, carry):
        body(pl.ds(pl.multiple_of(i * block, block), block))
        return carry

    lax.fori_loop(0, n_rows // block, step, 0, unroll=unroll)


def _fill_rms_scale(n_rows, read, r_ref):
    def body(rows):
        x = read(rows)
        ms = jnp.mean(x * x, axis=-1, keepdims=True)
        r_ref[rows, :] = jnp.broadcast_to(lax.rsqrt(ms + EPS), (SUBLANES, LANES))

    _row_loop(n_rows, SUBLANES, body, unroll=32)


def _mods_kernel(cond_ref, w_ref, b_ref, o_ref):
    s = _silu(cond_ref[...]).astype(BF16)
    o_ref[0] = _dot(s, w_ref[0].astype(BF16)) + b_ref[0]


def _mods_call(cond8, mod_w, mod_b):
    tn = 1024
    n = mod_w.shape[-1]
    return pl.pallas_call(
        _mods_kernel,
        grid=(DEPTH, n // tn),
        in_specs=[
            pl.BlockSpec((N_MODS, D_MODEL), lambda l, j: (0, 0)),
            pl.BlockSpec((1, D_MODEL, tn), lambda l, j: (l, 0, j)),
            pl.BlockSpec((1, 1, tn), lambda l, j: (l, 0, j)),
        ],
        out_specs=pl.BlockSpec((1, N_MODS, tn), lambda l, j: (l, 0, j)),
        out_shape=jax.ShapeDtypeStruct((DEPTH, N_MODS, n), F32),
        compiler_params=_cparams(("arbitrary", "arbitrary"), 32),
        name="mods",
    )(cond8, mod_w, mod_b.reshape(DEPTH, 1, n))


O_GLU = 2 * D_CONV
O_Q = O_GLU + Q_RANK
O_KV = O_Q + KV_RANK
O_END = O_KV + 2 * ROPE_DIM


def _even_in_kernel(rope, x_ref, mods_ref, g_ref, w_ref, gq_ref, gkv_ref, wq_ref, *rest):
    if rope:
        wqs_ref, cq_ref, sq_ref, ck_ref, sk_ref, a_ref, q_ref, ckv_ref, kpe_ref = rest
    else:
        a_ref, q_ref, ckv_ref, kpe_ref = rest
    h = _modnorm(x_ref[...], g_ref[...], mods_ref[0], 0).astype(BF16)
    lin = _dot(h, w_ref[:, :D_CONV])
    gate = _dot(h, w_ref[:, D_CONV:O_GLU])
    a_ref[...] = lin * (1.0 / (1.0 + jnp.exp(-gate)))
    qn = _rms(_dot(h, w_ref[:, O_GLU:O_Q]), gq_ref[...]).astype(BF16)
    ckv_ref[...] = _rms(_dot(h, w_ref[:, O_Q:O_KV]), gkv_ref[...])
    kp = _dot(h, w_ref[:, O_KV:O_END])
    q = _dot(qn, wq_ref[...]) * EXP2_SCALE
    if rope:
        kpe_ref[...] = kp[:, :ROPE_DIM] * ck_ref[...] + kp[:, ROPE_DIM:] * sk_ref[...]
        qs = _dot(qn, wqs_ref[...]) * EXP2_SCALE
        cq = cq_ref[...]
        sq = sq_ref[...]
        for hd in range(MLA_HEADS):
            c0 = hd * HEAD_PAD
            q_ref[:, c0:c0 + LANES] = q[:, c0:c0 + LANES].astype(BF16)
            q_ref[:, c0 + LANES:c0 + HEAD_PAD] = (
                q[:, c0 + LANES:c0 + HEAD_PAD] * cq + qs[:, hd * LANES:(hd + 1) * LANES] * sq).astype(BF16)
    else:
        kpe_ref[...] = kp[:, :ROPE_DIM]
        q_ref[...] = q.astype(BF16)


def _even_in_call(x, mods, rows_per_cond, g_mix, w1, g_qa, g_kva, wq, rope_args, tm=256):
    r = x.shape[0]
    rope = rope_args is not None
    const = lambda i: (0, 0)
    row = lambda i: (i, 0)
    in_specs = [
        pl.BlockSpec((tm, D_MODEL), row),
        pl.BlockSpec((1, N_MODS, D_MODEL), lambda i: ((i * tm) // rows_per_cond, 0, 0)),
        pl.BlockSpec((1, D_MODEL), const),
        pl.BlockSpec(w1.shape, const),
        pl.BlockSpec((1, Q_RANK), const),
        pl.BlockSpec((1, KV_RANK), const),
        pl.BlockSpec(wq.shape, const),
    ]
    args = [x, mods, g_mix, w1, g_qa, g_kva, wq]
    if rope:
        wqs, cq, sq, ck, sk = rope_args
        t_blocks = cq.shape[0] // tm
        pos = lambda i: (i % t_blocks, 0)
        in_specs += [
            pl.BlockSpec(wqs.shape, const),
            pl.BlockSpec((tm, LANES), pos),
            pl.BlockSpec((tm, LANES), pos),
            pl.BlockSpec((tm, ROPE_DIM), pos),
            pl.BlockSpec((tm, ROPE_DIM), pos),
        ]
        args += [wqs, cq, sq, ck, sk]
    return pl.pallas_call(
        functools.partial(_even_in_kernel, rope),
        grid=(r // tm,),
        in_specs=in_specs,
        out_specs=[
            pl.BlockSpec((tm, D_CONV), row),
            pl.BlockSpec((tm, MLA_HEADS * HEAD_PAD), row),
            pl.BlockSpec((tm, KV_RANK), row),
            pl.BlockSpec((tm, ROPE_DIM), row),
        ],
        out_shape=[
            jax.ShapeDtypeStruct((r, D_CONV), F32),
            jax.ShapeDtypeStruct((r, MLA_HEADS * HEAD_PAD), BF16),
            jax.ShapeDtypeStruct((r, KV_RANK), F32),
            jax.ShapeDtypeStruct((r, ROPE_DIM), F32),
        ],
        compiler_params=_cparams(("arbitrary",), 56),
        name="even_in",
    )(*args)


CONV_ROWS = 64
SHIFT_ROWS = 40


def _conv_kernel(tt, n_t, cur_ref, prev_ref, next_ref, w_ref, b_ref, g_ref, beta_ref, o_ref, buf_ref, sh_ref, y_ref):
    i = pl.program_id(1)
    keep_prev = (i > 0).astype(F32)
    keep_next = (i < n_t - 1).astype(F32)
    buf_ref[0:CONV_HALO, :] = prev_ref[0] * keep_prev
    buf_ref[CONV_HALO:CONV_HALO + tt, :] = cur_ref[0]
    buf_ref[CONV_HALO + tt:, :] = next_ref[0] * keep_next
    lane_tiles = [slice(c * LANES, (c + 1) * LANES) for c in range(D_CONV // LANES)]

    def shift(rows):
        for cs in lane_tiles:
            win = buf_ref[pl.ds(rows.start, SHIFT_ROWS + SUBLANES), cs]
            for r in range(1, SUBLANES):
                sh_ref[r - 1, rows, cs] = win[r:r + SHIFT_ROWS, :]

    _row_loop(tt + 2 * CONV_HALO - SUBLANES, SHIFT_ROWS, shift)

    first = CONV_HALO - CONV_PAD
    n_win = CONV_ROWS + (first + CONV_WIDTH - 1) // SUBLANES * SUBLANES

    def taps(rows):
        win_rows = pl.ds(rows.start, n_win)
        for cs in lane_tiles:
            acc = b_ref[:, cs]
            for r in range(SUBLANES):
                win = buf_ref[win_rows, cs] if r == 0 else sh_ref[r - 1, win_rows, cs]
                for q in range(n_win // SUBLANES):
                    k = q * SUBLANES + r - first
                    if 0 <= k < CONV_WIDTH:
                        acc = acc + win[q * SUBLANES:q * SUBLANES + CONV_ROWS, :] * w_ref[k:k + 1, cs]
            y_ref[rows, cs] = acc

    _row_loop(tt, CONV_ROWS, taps)
    o_ref[0] = _silu(_layernorm(y_ref[...], g_ref[...], beta_ref[...])).astype(BF16)


def _conv_call(a, batch, t, conv_w, conv_b, ln_g, ln_b, tt):
    n_t = t // tt
    assert tt % CONV_ROWS == 0 and (tt + 2 * CONV_HALO - SUBLANES) % SHIFT_ROWS == 0
    a3 = a.reshape(batch, t, D_CONV)
    hb = tt // CONV_HALO
    n_hb = t // CONV_HALO
    const = lambda b, i: (0, 0)
    out = pl.pallas_call(
        functools.partial(_conv_kernel, tt, n_t),
        grid=(batch, n_t),
        in_specs=[
            pl.BlockSpec((1, tt, D_CONV), lambda b, i: (b, i, 0)),
            pl.BlockSpec((1, CONV_HALO, D_CONV), lambda b, i: (b, jnp.maximum(i * hb - 1, 0), 0)),
            pl.BlockSpec((1, CONV_HALO, D_CONV), lambda b, i: (b, jnp.minimum((i + 1) * hb, n_hb - 1), 0)),
            pl.BlockSpec((CONV_WIDTH, D_CONV), const),
            pl.BlockSpec((1, D_CONV), const),
            pl.BlockSpec((1, D_CONV), const),
            pl.BlockSpec((1, D_CONV), const),
        ],
        out_specs=pl.BlockSpec((1, tt, D_CONV), lambda b, i: (b, i, 0)),
        out_shape=jax.ShapeDtypeStruct((batch, t, D_CONV), BF16),
        scratch_shapes=[
            pltpu.VMEM((tt + 2 * CONV_HALO, D_CONV), F32),
            pltpu.VMEM((SUBLANES - 1, tt + 2 * CONV_HALO - SUBLANES, D_CONV), F32),
            pltpu.VMEM((tt, D_CONV), F32),
        ],
        compiler_params=_cparams(("arbitrary", "arbitrary"), 32),
        name="conv",
    )(a3, a3, a3, conv_w, conv_b, ln_g, ln_b)
    return out.reshape(batch * t, D_CONV)


KV_ROWS = 1024
Q_TILE = 256
KEY_PIECE = 1024
NT_DIMS = (((1,), (1,)), ((), ()))


def _rope_key_pad(kpe):
    return jnp.concatenate([kpe, jnp.zeros((kpe.shape[0], HEAD_PAD - QK_DIM), F32)], axis=-1).astype(BF16)


def _fill_kv(k_ref, vt_ref, row0, ckv, kpe, wk, wvt):
    c = ckv.astype(BF16)
    n = c.shape[0]
    k_ref[row0:row0 + n, 0:NOPE_DIM] = _dot(c, wk).astype(BF16)
    k_ref[row0:row0 + n, NOPE_DIM:HEAD_PAD] = _rope_key_pad(kpe)
    vt_ref[:, row0:row0 + n] = lax.dot_general(wvt, c, NT_DIMS, preferred_element_type=F32).astype(BF16)


def _attn_kernel(t, past, q_ref, ckv_ref, kpe_ref, wk_ref, wvt_ref, cckv_ref, ckpe_ref, o_ref, k_ref, vt_ref, s_ref):
    @pl.when(pl.program_id(2) == 0)
    def _():
        wk = wk_ref[...]
        wvt = wvt_ref[...]
        _fill_kv(k_ref, vt_ref, 0, cckv_ref[0], ckpe_ref[0], wk, wvt)
        for r0 in range(0, t, KV_ROWS):
            _fill_kv(k_ref, vt_ref, past + r0, ckv_ref[r0:r0 + KV_ROWS, :], kpe_ref[r0:r0 + KV_ROWS, :], wk, wvt)

    n_tiles = q_ref.shape[0] // Q_TILE
    tk = k_ref.shape[0]
    pieces = [slice(k0, min(k0 + KEY_PIECE, tk)) for k0 in range(0, tk, KEY_PIECE)]

    def add(a, b):
        return b if a is None else a + b

    def score_piece(j, p, mx):
        q = q_ref[j * Q_TILE:(j + 1) * Q_TILE, :]
        s_ref[j % 2, pieces[p], :] = lax.dot_general(
            k_ref[pieces[p], :], q, NT_DIMS, preferred_element_type=F32)
        pm = jnp.max(s_ref[j % 2, pieces[p], :], axis=0, keepdims=True)
        return pm if mx is None else jnp.maximum(mx, pm)

    def exp_piece(j, p, m):
        e = jnp.exp2(s_ref[j % 2, pieces[p], :] - m)
        return e.astype(BF16), jnp.sum(e, axis=0, keepdims=True)

    def value_piece(p, e):
        return _dot(vt_ref[:, pieces[p]], e)

    m_next = None
    for p in range(len(pieces)):
        m_next = score_piece(0, p, m_next)
    for j in range(n_tiles):
        m, m_next = m_next, None
        acc = l = e_prev = None
        for p in range(len(pieces)):
            if j + 1 < n_tiles:
                m_next = score_piece(j + 1, p, m_next)
            if e_prev is not None:
                acc = add(acc, value_piece(p - 1, e_prev))
            e_prev, ls = exp_piece(j, p, m)
            l = add(l, ls)
        acc = add(acc, value_piece(len(pieces) - 1, e_prev))
        o_ref[j * Q_TILE:(j + 1) * Q_TILE, :] = (acc * (1.0 / l)).T.astype(BF16)


def _attn_call(q, ckv, kpe, wkv, wkv_t, batch, t, cache, tq):
    nq = t // tq
    past = cache[0].shape[1]
    tk = past + t
    assert t % KV_ROWS == 0 and past % LANES == 0 and tq % Q_TILE == 0
    return pl.pallas_call(
        functools.partial(_attn_kernel, t, past),
        grid=(batch, MLA_HEADS, nq),
        in_specs=[
            pl.BlockSpec((tq, HEAD_PAD), lambda b, h, i: (b * nq + i, h)),
            pl.BlockSpec((t, KV_RANK), lambda b, h, i: (b, 0)),
            pl.BlockSpec((t, ROPE_DIM), lambda b, h, i: (b, 0)),
            pl.BlockSpec((KV_RANK, NOPE_DIM), lambda b, h, i: (0, 2 * h)),
            pl.BlockSpec((V_DIM, KV_RANK), lambda b, h, i: (2 * h + 1, 0)),
            pl.BlockSpec((1, past, KV_RANK), lambda b, h, i: (b, 0, 0)),
            pl.BlockSpec((1, past, ROPE_DIM), lambda b, h, i: (b, 0, 0)),
        ],
        out_specs=pl.BlockSpec((tq, V_DIM), lambda b, h, i: (b * nq + i, h)),
        out_shape=jax.ShapeDtypeStruct((batch * t, MLA_HEADS * V_DIM), BF16),
        scratch_shapes=[
            pltpu.VMEM((tk, HEAD_PAD), BF16),
            pltpu.VMEM((V_DIM, tk), BF16),
            pltpu.VMEM((2, tk, Q_TILE), F32),
        ],
        compiler_params=_cparams(("arbitrary", "arbitrary", "arbitrary"), 56),
        name="attention",
    )(q, ckv, kpe, wkv, wkv_t, *cache)


def _attn_seq_kernel(q_ref, ckv_ref, kpe_ref, wkv_ref, o_ref):
    kv = _dot(ckv_ref[...].astype(BF16), wkv_ref[...])
    kpad = _rope_key_pad(kpe_ref[...])
    for hd in range(MLA_HEADS):
        c0 = hd * (NOPE_DIM + V_DIM)
        k = jnp.concatenate([kv[:, c0:c0 + NOPE_DIM].astype(BF16), kpad], axis=-1)
        s = lax.dot_general(q_ref[:, hd * HEAD_PAD:(hd + 1) * HEAD_PAD], k, NT_DIMS, preferred_element_type=F32)
        m = jnp.max(s, axis=-1, keepdims=True)
        e = jnp.exp2(s - m)
        l = jnp.sum(e, axis=-1, keepdims=True)
        o = _dot(e.astype(BF16), kv[:, c0 + NOPE_DIM:c0 + NOPE_DIM + V_DIM].astype(BF16)) * (1.0 / l)
        o_ref[:, hd * V_DIM:(hd + 1) * V_DIM] = o.astype(BF16)


def _attn_seq_call(q, ckv, kpe, wkv, batch, t):
    row = lambda b: (b, 0)
    return pl.pallas_call(
        _attn_seq_kernel,
        grid=(batch,),
        in_specs=[
            pl.BlockSpec((t, MLA_HEADS * HEAD_PAD), row),
            pl.BlockSpec((t, KV_RANK), row),
            pl.BlockSpec((t, ROPE_DIM), row),
            pl.BlockSpec(wkv.shape, lambda b: (0, 0)),
        ],
        out_specs=pl.BlockSpec((t, MLA_HEADS * V_DIM), row),
        out_shape=jax.ShapeDtypeStruct((batch * t, MLA_HEADS * V_DIM), BF16),
        compiler_params=_cparams(("arbitrary",), 32),
        name="attention_seq",
    )(q, ckv, kpe, wkv)


def _out_proj_kernel(n_in, x_ref, mods_ref, *rest):
    in_refs = rest[:n_in]
    w_ref, o_ref = rest[n_in:]
    acc = None
    k0 = 0
    for r in in_refs:
        k = r.shape[-1]
        part = _dot(r[...], w_ref[k0:k0 + k, :])
        acc = part if acc is None else acc + part
        k0 += k
    o_ref[...] = x_ref[...] + mods_ref[0][2:3, :] * acc


def _out_proj_call(x, mods, rows_per_cond, parts, w_o, tm=512):
    r = x.shape[0]
    row = lambda i: (i, 0)
    in_specs = [
        pl.BlockSpec((tm, D_MODEL), row),
        pl.BlockSpec((1, N_MODS, D_MODEL), lambda i: ((i * tm) // rows_per_cond, 0, 0)),
    ]
    in_specs += [pl.BlockSpec((tm, p.shape[-1]), row) for p in parts]
    in_specs += [pl.BlockSpec(w_o.shape, lambda i: (0, 0))]
    return pl.pallas_call(
        functools.partial(_out_proj_kernel, len(parts)),
        grid=(r // tm,),
        in_specs=in_specs,
        out_specs=pl.BlockSpec((tm, D_MODEL), row),
        out_shape=jax.ShapeDtypeStruct((r, D_MODEL), F32),
        compiler_params=_cparams(("arbitrary",), 48),
        name="out_proj",
    )(x, mods, *parts, w_o)


def _odd_in_kernel(tm, x_ref, mods_ref, g_ref, w_ref, lg_ref, lb_ref, ws_ref, bs_ref, o_ref):
    h = _modnorm(x_ref[...], g_ref[...], mods_ref[0], 0).astype(BF16)
    v = _gelu_tanh(_dot(h, w_ref[:, D_CMLP:]))
    v = _layernorm(v, lg_ref[...], lb_ref[...]).astype(BF16)
    u = _gelu_tanh(_dot(h, w_ref[:, :D_CMLP]))
    for ch in range(tm // CHUNK):
        rs = slice(ch * CHUNK, (ch + 1) * CHUNK)
        for g in range(CMLP_GROUPS):
            cs = slice(g * CMLP_GROUP_W, (g + 1) * CMLP_GROUP_W)
            mixed = _dot(ws_ref[g], v[rs, cs]) + bs_ref[:, cs]
            o_ref[rs, cs] = (u[rs, cs] * mixed).astype(BF16)


def _odd_in_call(x, mods, rows_per_cond, g_mix, w_in, ln_g, ln_b, w_s, b_s_full, tm=256):
    r = x.shape[0]
    const = lambda i: (0, 0)
    row = lambda i: (i, 0)
    return pl.pallas_call(
        functools.partial(_odd_in_kernel, tm),
        grid=(r // tm,),
        in_specs=[
            pl.BlockSpec((tm, D_MODEL), row),
            pl.BlockSpec((1, N_MODS, D_MODEL), lambda i: ((i * tm) // rows_per_cond, 0, 0)),
            pl.BlockSpec((1, D_MODEL), const),
            pl.BlockSpec(w_in.shape, const, pipeline_mode=pl.Buffered(1)),
            pl.BlockSpec((1, D_CMLP), const),
            pl.BlockSpec((1, D_CMLP), const),
            pl.BlockSpec(w_s.shape, lambda i: (0, 0, 0)),
            pl.BlockSpec((CHUNK, D_CMLP), const),
        ],
        out_specs=pl.BlockSpec((tm, D_CMLP), row),
        out_shape=jax.ShapeDtypeStruct((r, D_CMLP), BF16),
        compiler_params=_cparams(("arbitrary",), 56),
        name="odd_in",
    )(x, mods, g_mix, w_in, ln_g, ln_b, w_s, b_s_full)


def _ffn_kernel(final, x_ref, mods_ref, g_ref, wg0_ref, wg1_ref, wu0_ref, wu1_ref, wd0_ref, wd1_ref, *rest):
    if final:
        fg_ref, o_ref, h_ref, gs_ref, r_ref = rest
    else:
        o_ref, h_ref, gs_ref, r_ref = rest
    j = pl.program_id(1)
    tm = x_ref.shape[0]
    lane_tiles = [slice(c * LANES, (c + 1) * LANES) for c in range(D_MODEL // LANES)]

    @pl.when(j == 0)
    def _():
        gs_ref[0:1, :] = g_ref[...] * (1.0 + mods_ref[0][4:5, :])
        gs_ref[1:2, :] = mods_ref[0][3:4, :]
        gs_ref[2:3, :] = mods_ref[0][5:6, :]
        _fill_rms_scale(tm, lambda rows: x_ref[rows, :], r_ref)

        def prologue(rows):
            r = r_ref[rows, :]
            for cs in lane_tiles:
                h_ref[rows, cs] = (x_ref[rows, cs] * r * gs_ref[0:1, cs] + gs_ref[1:2, cs]).astype(BF16)

        _row_loop(tm, NORM_ROWS, prologue, unroll=2)

    half = D_MODEL // 2

    def accumulate(base_ref):
        h0 = h_ref[:, :half]
        h1 = h_ref[:, half:]
        gate = _dot(h0, wg0_ref[...].astype(BF16)) + _dot(h1, wg1_ref[...].astype(BF16))
        up = _dot(h0, wu0_ref[...].astype(BF16)) + _dot(h1, wu1_ref[...].astype(BF16))
        hid = (_silu(gate) * up).astype(BF16)
        for cs, wd_ref in ((slice(0, half), wd0_ref), (slice(half, D_MODEL), wd1_ref)):
            o_ref[:, cs] = base_ref[:, cs] + gs_ref[2:3, cs] * _dot(hid, wd_ref[...].astype(BF16))

    @pl.when(j == 0)
    def _():
        accumulate(x_ref)

    @pl.when(j > 0)
    def _():
        accumulate(o_ref)

    if final:
        @pl.when(j == pl.num_programs(1) - 1)
        def _():
            _fill_rms_scale(tm, lambda rows: o_ref[rows, :], r_ref)

            def final_norm(rows):
                r = r_ref[rows, :]
                for cs in lane_tiles:
                    o_ref[rows, cs] = o_ref[rows, cs] * r * fg_ref[:, cs]

            _row_loop(tm, NORM_ROWS, final_norm, unroll=2)


def _ffn_call(x, mods, rows_per_cond, g_ffn, layer, wg, wu, wd, final_g=None, tm=1024, tn=256):
    r = x.shape[0]
    final = final_g is not None
    const = lambda i, j: (0, 0)
    row = lambda i, j: (i, 0)
    in_specs = [
        pl.BlockSpec((tm, D_MODEL), row),
        pl.BlockSpec((1, N_MODS, D_MODEL), lambda i, j: ((i * tm) // rows_per_cond, 0, 0)),
        pl.BlockSpec((1, D_MODEL), const),
        pl.BlockSpec((None, D_MODEL // 2, tn), lambda i, j: (layer, 0, j)),
        pl.BlockSpec((None, D_MODEL // 2, tn), lambda i, j: (layer, 1, j)),
        pl.BlockSpec((None, D_MODEL // 2, tn), lambda i, j: (layer, 0, j)),
        pl.BlockSpec((None, D_MODEL // 2, tn), lambda i, j: (layer, 1, j)),
        pl.BlockSpec((None, tn, D_MODEL // 2), lambda i, j: (layer, j, 0)),
        pl.BlockSpec((None, tn, D_MODEL // 2), lambda i, j: (layer, j, 1)),
    ]
    args = [x, mods, g_ffn, wg, wg, wu, wu, wd, wd]
    if final:
        in_specs.append(pl.BlockSpec((1, D_MODEL), const))
        args.append(final_g)
    return pl.pallas_call(
        functools.partial(_ffn_kernel, final),
        grid=(r // tm, D_FF // tn),
        in_specs=in_specs,
        out_specs=pl.BlockSpec((tm, D_MODEL), row),
        out_shape=jax.ShapeDtypeStruct((r, D_MODEL), F32),
        scratch_shapes=[
            pltpu.VMEM((tm, D_MODEL), BF16),
            pltpu.VMEM((SUBLANES, D_MODEL), F32),
            pltpu.VMEM((tm, LANES), F32),
        ],
        compiler_params=_cparams(("arbitrary", "arbitrary"), 60),
        name="ffn",
    )(*args)


def _rope_tables(n_tokens):
    n_freq = ROPE_DIM // 4
    lane = np.arange(ROPE_DIM)
    tok = np.arange(n_tokens)
    pos = np.where(lane[None, :] // (2 * n_freq) == 0, tok[:, None] // GRID_W, tok[:, None] % GRID_W).astype(np.float32)
    exponent = -np.arange(n_freq, dtype=np.float32) * np.float32(2.0) / np.float32(ROPE_DIM // 2)
    inv_freq = np.power(np.float32(ROPE_THETA), exponent).astype(np.float32)
    ang = pos * inv_freq[lane % n_freq][None, :]
    sign = np.where((lane // n_freq) % 2 == 0, -1.0, 1.0).astype(np.float32)
    return np.cos(ang).astype(np.float32), (np.sin(ang) * sign[None, :]).astype(np.float32)


def _pair_swap(w):
    perm = jnp.arange(ROPE_DIM) ^ (ROPE_DIM // 4)
    return w[..., perm]


def kernel(x_prompt, x_sample, cache_ckv, cache_kpe, c, c_ctx, mod_w, mod_b, norm_mix_g, norm_ffn_g, ffn_w_gate, ffn_w_up, ffn_w_down, ev_w_in, ev_conv_w, ev_conv_b, ev_conv_ln_g, ev_conv_ln_b, ev_q_norm_g, ev_w_qb, ev_kv_norm_g, ev_w_kvb, ev_w_o, od_w_in, od_ln_g, od_ln_b, od_w_s, od_b_s, od_w_o, final_norm_g):
    bp, tp, _ = x_prompt.shape
    bs, ts, _ = x_sample.shape
    xp = x_prompt.reshape(bp * tp, D_MODEL)
    xs = x_sample.reshape(bs * ts, D_MODEL)

    cond8 = jnp.concatenate([c_ctx[None, :], c, jnp.zeros((N_MODS - 1 - bs, D_MODEL), F32)], axis=0)
    mods = _mods_call(cond8, mod_w, mod_b)
    mods = mods[:, :1 + bs].reshape(DEPTH, 1 + bs, 6, D_MODEL)
    mods = jnp.pad(mods, ((0, 0), (0, 0), (0, N_MODS - 6), (0, 0)))

    cos_k, sin_k = _rope_tables(ts)
    zpad = np.zeros((ts, LANES - ROPE_DIM), np.float32)
    cos_q = np.concatenate([cos_k, zpad], axis=-1)
    sin_q = np.concatenate([sin_k, zpad], axis=-1)

    streams = [
        dict(x=xp, batch=bp, t=tp, rows=bp * tp, conds=slice(0, 1), rope=False),
        dict(x=xs, batch=bs, t=ts, rows=ts, conds=slice(1, 1 + bs), rope=True),
    ]
    state = {}
    for l in range(DEPTH):
        g_mix = norm_mix_g[l][None, :]
        g_ffn = norm_ffn_g[l][None, :]
        last = l == DEPTH - 1
        if l % 2 == 0:
            e = l // 2
            w_in = ev_w_in[e]
            w1 = jnp.concatenate([w_in, _pair_swap(w_in[:, O_KV:])], axis=1).astype(BF16)
            wq3 = ev_w_qb[e].reshape(Q_RANK, MLA_HEADS, QK_DIM)
            wq = jnp.pad(wq3, ((0, 0), (0, 0), (0, HEAD_PAD - QK_DIM))).reshape(Q_RANK, -1).astype(BF16)
            wqs = jnp.pad(_pair_swap(wq3[:, :, NOPE_DIM:]), ((0, 0), (0, 0), (0, LANES - ROPE_DIM)))
            wqs = wqs.reshape(Q_RANK, -1).astype(BF16)
            wkv = ev_w_kvb[e].astype(BF16)
            wkv_t = ev_w_kvb[e].T.astype(BF16)
            w_o = ev_w_o[e].astype(BF16)
            conv_w = ev_conv_w[e].reshape(CONV_WIDTH, D_CONV)
            for s in streams:
                m = mods[l, s["conds"]]
                rope_args = (wqs, cos_q, sin_q, cos_k, sin_k) if s["rope"] else None
                a, q, ckv, kpe = _even_in_call(s["x"], m, s["rows"], g_mix, w1, ev_q_norm_g[e][None, :],
                                               ev_kv_norm_g[e][None, :], wq, rope_args)
                a = _conv_call(a, s["batch"], s["t"], conv_w, ev_conv_b[e][None, :],
                               ev_conv_ln_g[e][None, :], ev_conv_ln_b[e][None, :], tt=min(s["t"], 256))
                if s["rope"]:
                    att = _attn_call(q, ckv, kpe, wkv, wkv_t, s["batch"], s["t"],
                                     (cache_ckv[:, e], cache_kpe[:, e]), tq=2048)
                else:
                    att = _attn_seq_call(q, ckv, kpe, wkv, s["batch"], s["t"])
                    state.setdefault("ckv", []).append(ckv.reshape(bp, tp, KV_RANK))
                    state.setdefault("kpe", []).append(kpe.reshape(bp, tp, ROPE_DIM))
                s["x"] = _out_proj_call(s["x"], m, s["rows"], [a, att], w_o)
        else:
            o = l // 2
            w_in = od_w_in[o].astype(BF16)
            w_o = od_w_o[o].astype(BF16)
            w_s = od_w_s[o].astype(BF16)
            b_s_full = jnp.repeat(od_b_s[o].T, CMLP_GROUP_W, axis=1)
            for s in streams:
                m = mods[l, s["conds"]]
                p = _odd_in_call(s["x"], m, s["rows"], g_mix, w_in, od_ln_g[o][None, :], od_ln_b[o][None, :],
                                 w_s, b_s_full)
                s["x"] = _out_proj_call(s["x"], m, s["rows"], [p], w_o)
        for s in streams:
            m = mods[l, s["conds"]]
            s["x"] = _ffn_call(s["x"], m, s["rows"], g_ffn, l, ffn_w_gate, ffn_w_up, ffn_w_down,
                               final_g=final_norm_g[None, :] if last else None)

    y_prompt = streams[0]["x"].reshape(bp, tp, D_MODEL)
    y_sample = streams[1]["x"].reshape(bs, ts, D_MODEL)
    state_ckv = jnp.stack(state["ckv"], axis=1)
    state_kpe = jnp.stack(state["kpe"], axis=1)
    return (y_prompt, y_sample, state_ckv, state_kpe)
```

```python
import functools

import jax
import jax.numpy as jnp
import numpy as np
from jax import lax
from jax.experimental import pallas as pl
from jax.experimental.pallas import tpu as pltpu

D_MODEL = 2048
DEPTH = 2
GRID_W = 64
D_CONV = D_MODEL // 2
CONV_WIDTH = 31
CONV_PAD = CONV_WIDTH // 2
MLA_HEADS = D_MODEL // 256
Q_RANK = D_MODEL // 4
KV_RANK = D_MODEL // 8
NOPE_DIM = 128
ROPE_DIM = 64
V_DIM = 128
QK_DIM = NOPE_DIM + ROPE_DIM
ROPE_THETA = 10000.0
D_CMLP = D_MODEL
CHUNK = 128
CMLP_GROUPS = 8
CMLP_GROUP_W = D_CMLP // CMLP_GROUPS
D_FF = 256 * (-(-8 * D_MODEL // (3 * 256)))
ATTN_SCALE = QK_DIM ** -0.5
EXP2_SCALE = ATTN_SCALE * 1.4426950408889634
EPS = 1e-6

LANES = 128
SUBLANES = 8
HEAD_PAD = 2 * LANES
CONV_HALO = 2 * SUBLANES
N_MODS = 8
MIB = 1024 * 1024

BF16 = jnp.bfloat16
F32 = jnp.float32


VMEM_LIMIT_MIB = dict(mods=32, even_in=56, conv=32, attention=56, attention_seq=32, out_proj=48, odd_in=56, ffn=60)


def _cparams(name, grid_rank):
    return pltpu.CompilerParams(dimension_semantics=("arbitrary",) * grid_rank,
                                vmem_limit_bytes=VMEM_LIMIT_MIB[name] * MIB)


def _silu(x):
    return x * (1.0 / (1.0 + jnp.exp(-x)))


def _gelu_tanh(x):
    cdf = 0.5 * (1.0 + jnp.tanh(0.7978845608028654 * (x + 0.044715 * (x * x * x))))
    return x * cdf


def _rms(x, g):
    return (x * lax.rsqrt(jnp.mean(x * x, axis=-1, keepdims=True) + EPS)) * g


def _layernorm(x, g, b):
    mu = jnp.mean(x, axis=-1, keepdims=True)
    xc = x - mu
    var = jnp.mean(xc * xc, axis=-1, keepdims=True)
    return xc * lax.rsqrt(var + EPS) * g + b


def _modnorm(x, g, mods, row):
    return _rms(x, g) * (1.0 + mods[row + 1:row + 2, :]) + mods[row:row + 1, :]


def _dot(a, b):
    return jnp.dot(a, b, preferred_element_type=F32)


NORM_ROWS = 2 * SUBLANES


def _row_loop(n_rows, block, body, unroll=1):
    def step(i, carry):
        body(pl.ds(pl.multiple_of(i * block, block), block))
        return carry

    lax.fori_loop(0, n_rows // block, step, 0, unroll=unroll)


def _fill_rms_scale(n_rows, read, r_ref):
    def body(rows):
        x = read(rows)
        ms = jnp.mean(x * x, axis=-1, keepdims=True)
        r_ref[rows, :] = jnp.broadcast_to(lax.rsqrt(ms + EPS), (SUBLANES, LANES))

    _row_loop(n_rows, SUBLANES, body, unroll=32)


def _mods_kernel(cond_ref, w_ref, b_ref, o_ref):
    s = _silu(cond_ref[...]).astype(BF16)
    o_ref[0] = _dot(s, w_ref[0].astype(BF16)) + b_ref[0]


def _mods_call(cond8, mod_w, mod_b):
    tn = 1024
    n = mod_w.shape[-1]
    return pl.pallas_call(
        _mods_kernel,
        grid=(DEPTH, n // tn),
        in_specs=[
            pl.BlockSpec((N_MODS, D_MODEL), lambda l, j: (0, 0)),
            pl.BlockSpec((1, D_MODEL, tn), lambda l, j: (l, 0, j)),
            pl.BlockSpec((1, 1, tn), lambda l, j: (l, 0, j)),
        ],
        out_specs=pl.BlockSpec((1, N_MODS, tn), lambda l, j: (l, 0, j)),
        out_shape=jax.ShapeDtypeStruct((DEPTH, N_MODS, n), F32),
        compiler_params=_cparams("mods", 2),
        name="mods",
    )(cond8, mod_w, mod_b.reshape(DEPTH, 1, n))


O_GLU = 2 * D_CONV
O_Q = O_GLU + Q_RANK
O_KV = O_Q + KV_RANK


def _even_in_kernel(rope, x_ref, mods_ref, g_ref, w_ref, wk_ref, gq_ref, gkv_ref, wq_ref, *rest):
    if rope:
        wqs_ref, cq_ref, sq_ref, ck_ref, sk_ref, a_ref, q_ref, ckv_ref, kpe_ref = rest
    else:
        a_ref, q_ref, ckv_ref, kpe_ref = rest
    h = _modnorm(x_ref[...], g_ref[...], mods_ref[0], 0).astype(BF16)
    lin = _dot(h, w_ref[:, :D_CONV])
    gate = _dot(h, w_ref[:, D_CONV:O_GLU])
    a_ref[...] = lin * (1.0 / (1.0 + jnp.exp(-gate)))
    qn = _rms(_dot(h, w_ref[:, O_GLU:O_Q]), gq_ref[...]).astype(BF16)
    ckv_ref[...] = _rms(_dot(h, w_ref[:, O_Q:O_KV]), gkv_ref[...])
    kp = _dot(h, wk_ref[...])
    q = _dot(qn, wq_ref[...]) * EXP2_SCALE
    if rope:
        kpe_ref[...] = kp[:, :ROPE_DIM] * ck_ref[...] + kp[:, ROPE_DIM:] * sk_ref[...]
        qs = _dot(qn, wqs_ref[...]) * EXP2_SCALE
        cq = cq_ref[...]
        sq = sq_ref[...]
        for hd in range(MLA_HEADS):
            c0 = hd * HEAD_PAD
            q_ref[:, c0:c0 + LANES] = q[:, c0:c0 + LANES].astype(BF16)
            q_ref[:, c0 + LANES:c0 + HEAD_PAD] = (
                q[:, c0 + LANES:c0 + HEAD_PAD] * cq + qs[:, hd * LANES:(hd + 1) * LANES] * sq).astype(BF16)
    else:
        kpe_ref[...] = kp[:, :ROPE_DIM]
        q_ref[...] = q.astype(BF16)


def _even_in_call(x, mods, rows_per_cond, g_mix, w1, wk2, g_qa, g_kva, wq, rope_args, tm=256):
    r = x.shape[0]
    rope = rope_args is not None
    const = lambda i: (0, 0)
    row = lambda i: (i, 0)
    in_specs = [
        pl.BlockSpec((tm, D_MODEL), row),
        pl.BlockSpec((1, N_MODS, D_MODEL), lambda i: ((i * tm) // rows_per_cond, 0, 0)),
        pl.BlockSpec((1, D_MODEL), const),
        pl.BlockSpec(w1.shape, const),
        pl.BlockSpec(wk2.shape, const),
        pl.BlockSpec((1, Q_RANK), const),
        pl.BlockSpec((1, KV_RANK), const),
        pl.BlockSpec(wq.shape, const),
    ]
    args = [x, mods, g_mix, w1, wk2, g_qa, g_kva, wq]
    if rope:
        wqs, cq, sq, ck, sk = rope_args
        t_blocks = cq.shape[0] // tm
        pos = lambda i: (i % t_blocks, 0)
        in_specs += [
            pl.BlockSpec(wqs.shape, const),
            pl.BlockSpec((tm, LANES), pos),
            pl.BlockSpec((tm, LANES), pos),
            pl.BlockSpec((tm, ROPE_DIM), pos),
            pl.BlockSpec((tm, ROPE_DIM), pos),
        ]
        args += [wqs, cq, sq, ck, sk]
    return pl.pallas_call(
        functools.partial(_even_in_kernel, rope),
        grid=(r // tm,),
        in_specs=in_specs,
        out_specs=[
            pl.BlockSpec((tm, D_CONV), row),
            pl.BlockSpec((tm, MLA_HEADS * HEAD_PAD), row),
            pl.BlockSpec((tm, KV_RANK), row),
            pl.BlockSpec((tm, ROPE_DIM), row),
        ],
        out_shape=[
            jax.ShapeDtypeStruct((r, D_CONV), F32),
            jax.ShapeDtypeStruct((r, MLA_HEADS * HEAD_PAD), BF16),
            jax.ShapeDtypeStruct((r, KV_RANK), F32),
            jax.ShapeDtypeStruct((r, ROPE_DIM), F32),
        ],
        compiler_params=_cparams("even_in", 1),
        name="even_in",
    )(*args)


CONV_ROWS = 64
SHIFT_ROWS = 40


def _conv_kernel(tt, n_t, cur_ref, prev_ref, next_ref, w_ref, b_ref, g_ref, beta_ref, o_ref, buf_ref, sh_ref, y_ref):
    i = pl.program_id(1)
    keep_prev = (i > 0).astype(F32)
    keep_next = (i < n_t - 1).astype(F32)
    buf_ref[0:CONV_HALO, :] = prev_ref[0] * keep_prev
    buf_ref[CONV_HALO:CONV_HALO + tt, :] = cur_ref[0]
    buf_ref[CONV_HALO + tt:, :] = next_ref[0] * keep_next
    lane_tiles = [slice(c * LANES, (c + 1) * LANES) for c in range(D_CONV // LANES)]

    def shift(rows):
        for cs in lane_tiles:
            win = buf_ref[pl.ds(rows.start, SHIFT_ROWS + SUBLANES), cs]
            for r in range(1, SUBLANES):
                sh_ref[r - 1, rows, cs] = win[r:r + SHIFT_ROWS, :]

    _row_loop(tt + 2 * CONV_HALO - SUBLANES, SHIFT_ROWS, shift)

    first = CONV_HALO - CONV_PAD
    n_win = CONV_ROWS + (first + CONV_WIDTH - 1) // SUBLANES * SUBLANES

    def taps(rows):
        win_rows = pl.ds(rows.start, n_win)
        for cs in lane_tiles:
            acc = b_ref[:, cs]
            for r in range(SUBLANES):
                win = buf_ref[win_rows, cs] if r == 0 else sh_ref[r - 1, win_rows, cs]
                for q in range(n_win // SUBLANES):
                    k = q * SUBLANES + r - first
                    if 0 <= k < CONV_WIDTH:
                        acc = acc + win[q * SUBLANES:q * SUBLANES + CONV_ROWS, :] * w_ref[k:k + 1, cs]
            y_ref[rows, cs] = acc

    _row_loop(tt, CONV_ROWS, taps)
    o_ref[0] = _silu(_layernorm(y_ref[...], g_ref[...], beta_ref[...])).astype(BF16)


def _conv_call(a, batch, t, conv_w, conv_b, ln_g, ln_b, tt):
    n_t = t // tt
    assert tt % CONV_ROWS == 0 and (tt + 2 * CONV_HALO - SUBLANES) % SHIFT_ROWS == 0
    a3 = a.reshape(batch, t, D_CONV)
    hb = tt // CONV_HALO
    n_hb = t // CONV_HALO
    const = lambda b, i: (0, 0)
    out = pl.pallas_call(
        functools.partial(_conv_kernel, tt, n_t),
        grid=(batch, n_t),
        in_specs=[
            pl.BlockSpec((1, tt, D_CONV), lambda b, i: (b, i, 0)),
            pl.BlockSpec((1, CONV_HALO, D_CONV), lambda b, i: (b, jnp.maximum(i * hb - 1, 0), 0)),
            pl.BlockSpec((1, CONV_HALO, D_CONV), lambda b, i: (b, jnp.minimum((i + 1) * hb, n_hb - 1), 0)),
            pl.BlockSpec((CONV_WIDTH, D_CONV), const),
            pl.BlockSpec((1, D_CONV), const),
            pl.BlockSpec((1, D_CONV), const),
            pl.BlockSpec((1, D_CONV), const),
        ],
        out_specs=pl.BlockSpec((1, tt, D_CONV), lambda b, i: (b, i, 0)),
        out_shape=jax.ShapeDtypeStruct((batch, t, D_CONV), BF16),
        scratch_shapes=[
            pltpu.VMEM((tt + 2 * CONV_HALO, D_CONV), F32),
            pltpu.VMEM((SUBLANES - 1, tt + 2 * CONV_HALO - SUBLANES, D_CONV), F32),
            pltpu.VMEM((tt, D_CONV), F32),
        ],
        compiler_params=_cparams("conv", 2),
        name="conv",
    )(a3, a3, a3, conv_w, conv_b, ln_g, ln_b)
    return out.reshape(batch * t, D_CONV)


KV_ROWS = 1024
Q_TILE = 256
KEY_PIECE = 1024
NT_DIMS = (((1,), (1,)), ((), ()))


def _rope_key_pad(kpe):
    return jnp.concatenate([kpe, jnp.zeros((kpe.shape[0], HEAD_PAD - QK_DIM), F32)], axis=-1).astype(BF16)


def _fill_kv(k_ref, vt_ref, row0, ckv, kpe, wk, wvt):
    c = ckv.astype(BF16)
    n = c.shape[0]
    k_ref[row0:row0 + n, 0:NOPE_DIM] = _dot(c, wk).astype(BF16)
    k_ref[row0:row0 + n, NOPE_DIM:HEAD_PAD] = _rope_key_pad(kpe)
    vt_ref[:, row0:row0 + n] = lax.dot_general(wvt, c, NT_DIMS, preferred_element_type=F32).astype(BF16)


def _attn_kernel(t, past, q_ref, ckv_ref, kpe_ref, wk_ref, wvt_ref, cckv_ref, ckpe_ref, o_ref, k_ref, vt_ref, s_ref):
    @pl.when(pl.program_id(2) == 0)
    def _():
        wk = wk_ref[...]
        wvt = wvt_ref[...]
        _fill_kv(k_ref, vt_ref, 0, cckv_ref[0], ckpe_ref[0], wk, wvt)
        for r0 in range(0, t, KV_ROWS):
            _fill_kv(k_ref, vt_ref, past + r0, ckv_ref[r0:r0 + KV_ROWS, :], kpe_ref[r0:r0 + KV_ROWS, :], wk, wvt)

    n_tiles = q_ref.shape[0] // Q_TILE
    tk = k_ref.shape[0]
    pieces = [slice(k0, min(k0 + KEY_PIECE, tk)) for k0 in range(0, tk, KEY_PIECE)]

    def add(a, b):
        return b if a is None else a + b

    def score_piece(j, p, mx):
        q = q_ref[j * Q_TILE:(j + 1) * Q_TILE, :]
        s_ref[j % 2, pieces[p], :] = lax.dot_general(
            k_ref[pieces[p], :], q, NT_DIMS, preferred_element_type=F32)
        pm = jnp.max(s_ref[j % 2, pieces[p], :], axis=0, keepdims=True)
        return pm if mx is None else jnp.maximum(mx, pm)

    def exp_piece(j, p, m):
        e = jnp.exp2(s_ref[j % 2, pieces[p], :] - m)
        return e.astype(BF16), jnp.sum(e, axis=0, keepdims=True)

    def value_piece(p, e):
        return _dot(vt_ref[:, pieces[p]], e)

    m_next = None
    for p in range(len(pieces)):
        m_next = score_piece(0, p, m_next)
    for j in range(n_tiles):
        m, m_next = m_next, None
        acc = l = e_prev = None
        for p in range(len(pieces)):
            if j + 1 < n_tiles:
                m_next = score_piece(j + 1, p, m_next)
            if e_prev is not None:
                acc = add(acc, value_piece(p - 1, e_prev))
            e_prev, ls = exp_piece(j, p, m)
            l = add(l, ls)
        acc = add(acc, value_piece(len(pieces) - 1, e_prev))
        o_ref[j * Q_TILE:(j + 1) * Q_TILE, :] = (acc * (1.0 / l)).T.astype(BF16)


def _attn_call(q, ckv, kpe, wkv, wkv_t, batch, t, cache, tq):
    nq = t // tq
    past = cache[0].shape[1]
    tk = past + t
    assert t % KV_ROWS == 0 and past % LANES == 0 and tq % Q_TILE == 0
    return pl.pallas_call(
        functools.partial(_attn_kernel, t, past),
        grid=(batch, MLA_HEADS, nq),
        in_specs=[
            pl.BlockSpec((tq, HEAD_PAD), lambda b, h, i: (b * nq + i, h)),
            pl.BlockSpec((t, KV_RANK), lambda b, h, i: (b, 0)),
            pl.BlockSpec((t, ROPE_DIM), lambda b, h, i: (b, 0)),
            pl.BlockSpec((KV_RANK, NOPE_DIM), lambda b, h, i: (0, 2 * h)),
            pl.BlockSpec((V_DIM, KV_RANK), lambda b, h, i: (2 * h + 1, 0)),
            pl.BlockSpec((1, past, KV_RANK), lambda b, h, i: (b, 0, 0)),
            pl.BlockSpec((1, past, ROPE_DIM), lambda b, h, i: (b, 0, 0)),
        ],
        out_specs=pl.BlockSpec((tq, V_DIM), lambda b, h, i: (b * nq + i, h)),
        out_shape=jax.ShapeDtypeStruct((batch * t, MLA_HEADS * V_DIM), BF16),
        scratch_shapes=[
            pltpu.VMEM((tk, HEAD_PAD), BF16),
            pltpu.VMEM((V_DIM, tk), BF16),
            pltpu.VMEM((2, tk, Q_TILE), F32),
        ],
        compiler_params=_cparams("attention", 3),
        name="attention",
    )(q, ckv, kpe, wkv, wkv_t, *cache)


def _attn_seq_kernel(q_ref, ckv_ref, kpe_ref, wkv_ref, o_ref):
    kv = _dot(ckv_ref[...].astype(BF16), wkv_ref[...])
    kpad = _rope_key_pad(kpe_ref[...])
    for hd in range(MLA_HEADS):
        c0 = hd * (NOPE_DIM + V_DIM)
        k = jnp.concatenate([kv[:, c0:c0 + NOPE_DIM].astype(BF16), kpad], axis=-1)
        s = lax.dot_general(q_ref[:, hd * HEAD_PAD:(hd + 1) * HEAD_PAD], k, NT_DIMS, preferred_element_type=F32)
        m = jnp.max(s, axis=-1, keepdims=True)
        e = jnp.exp2(s - m)
        l = jnp.sum(e, axis=-1, keepdims=True)
        o = _dot(e.astype(BF16), kv[:, c0 + NOPE_DIM:c0 + NOPE_DIM + V_DIM].astype(BF16)) * (1.0 / l)
        o_ref[:, hd * V_DIM:(hd + 1) * V_DIM] = o.astype(BF16)


def _attn_seq_call(q, ckv, kpe, wkv, batch, t):
    row = lambda b: (b, 0)
    return pl.pallas_call(
        _attn_seq_kernel,
        grid=(batch,),
        in_specs=[
            pl.BlockSpec((t, MLA_HEADS * HEAD_PAD), row),
            pl.BlockSpec((t, KV_RANK), row),
            pl.BlockSpec((t, ROPE_DIM), row),
            pl.BlockSpec(wkv.shape, lambda b: (0, 0)),
        ],
        out_specs=pl.BlockSpec((t, MLA_HEADS * V_DIM), row),
        out_shape=jax.ShapeDtypeStruct((batch * t, MLA_HEADS * V_DIM), BF16),
        compiler_params=_cparams("attention_seq", 1),
        name="attention_seq",
    )(q, ckv, kpe, wkv)


def _out_proj_kernel(n_in, x_ref, mods_ref, *rest):
    in_refs = rest[:n_in]
    w_ref, o_ref = rest[n_in:]
    acc = None
    k0 = 0
    for r in in_refs:
        k = r.shape[-1]
        part = _dot(r[...], w_ref[k0:k0 + k, :])
        acc = part if acc is None else acc + part
        k0 += k
    o_ref[...] = x_ref[...] + mods_ref[0][2:3, :] * acc


def _out_proj_call(x, mods, rows_per_cond, parts, w_o, tm=512):
    r = x.shape[0]
    row = lambda i: (i, 0)
    in_specs = [
        pl.BlockSpec((tm, D_MODEL), row),
        pl.BlockSpec((1, N_MODS, D_MODEL), lambda i: ((i * tm) // rows_per_cond, 0, 0)),
    ]
    in_specs += [pl.BlockSpec((tm, p.shape[-1]), row) for p in parts]
    in_specs += [pl.BlockSpec(w_o.shape, lambda i: (0, 0))]
    return pl.pallas_call(
        functools.partial(_out_proj_kernel, len(parts)),
        grid=(r // tm,),
        in_specs=in_specs,
        out_specs=pl.BlockSpec((tm, D_MODEL), row),
        out_shape=jax.ShapeDtypeStruct((r, D_MODEL), F32),
        compiler_params=_cparams("out_proj", 1),
        name="out_proj",
    )(x, mods, *parts, w_o)


def _odd_in_kernel(tm, x_ref, mods_ref, g_ref, w_ref, lg_ref, lb_ref, ws_ref, bs_ref, o_ref):
    h = _modnorm(x_ref[...], g_ref[...], mods_ref[0], 0).astype(BF16)
    v = _gelu_tanh(_dot(h, w_ref[:, D_CMLP:]))
    v = _layernorm(v, lg_ref[...], lb_ref[...]).astype(BF16)
    u = _gelu_tanh(_dot(h, w_ref[:, :D_CMLP]))
    for ch in range(tm // CHUNK):
        rs = slice(ch * CHUNK, (ch + 1) * CHUNK)
        for g in range(CMLP_GROUPS):
            cs = slice(g * CMLP_GROUP_W, (g + 1) * CMLP_GROUP_W)
            mixed = _dot(ws_ref[g], v[rs, cs]) + bs_ref[:, cs]
            o_ref[rs, cs] = (u[rs, cs] * mixed).astype(BF16)


def _odd_in_call(x, mods, rows_per_cond, g_mix, w_in, ln_g, ln_b, w_s, b_s_full, tm=256):
    r = x.shape[0]
    const = lambda i: (0, 0)
    row = lambda i: (i, 0)
    return pl.pallas_call(
        functools.partial(_odd_in_kernel, tm),
        grid=(r // tm,),
        in_specs=[
            pl.BlockSpec((tm, D_MODEL), row),
            pl.BlockSpec((1, N_MODS, D_MODEL), lambda i: ((i * tm) // rows_per_cond, 0, 0)),
            pl.BlockSpec((1, D_MODEL), const),
            pl.BlockSpec(w_in.shape, const, pipeline_mode=pl.Buffered(1)),
            pl.BlockSpec((1, D_CMLP), const),
            pl.BlockSpec((1, D_CMLP), const),
            pl.BlockSpec(w_s.shape, lambda i: (0, 0, 0)),
            pl.BlockSpec((CHUNK, D_CMLP), const),
        ],
        out_specs=pl.BlockSpec((tm, D_CMLP), row),
        out_shape=jax.ShapeDtypeStruct((r, D_CMLP), BF16),
        compiler_params=_cparams("odd_in", 1),
        name="odd_in",
    )(x, mods, g_mix, w_in, ln_g, ln_b, w_s, b_s_full)


def _ffn_kernel(final, x_ref, mods_ref, g_ref, wg_ref, wu_ref, wd_ref, *rest):
    if final:
        fg_ref, o_ref, h_ref, gs_ref, r_ref = rest
    else:
        o_ref, h_ref, gs_ref, r_ref = rest
    j = pl.program_id(1)
    tm = x_ref.shape[0]
    lane_tiles = [slice(c * LANES, (c + 1) * LANES) for c in range(D_MODEL // LANES)]

    @pl.when(j == 0)
    def _():
        gs_ref[0:1, :] = g_ref[...] * (1.0 + mods_ref[0][4:5, :])
        gs_ref[1:2, :] = mods_ref[0][3:4, :]
        gs_ref[2:3, :] = mods_ref[0][5:6, :]
        _fill_rms_scale(tm, lambda rows: x_ref[rows, :], r_ref)

        def prologue(rows):
            r = r_ref[rows, :]
            for cs in lane_tiles:
                h_ref[rows, cs] = (x_ref[rows, cs] * r * gs_ref[0:1, cs] + gs_ref[1:2, cs]).astype(BF16)

        _row_loop(tm, NORM_ROWS, prologue, unroll=2)

    def gated_down_proj():
        h = h_ref[...]
        hid = (_silu(_dot(h, wg_ref[...].astype(BF16))) * _dot(h, wu_ref[...].astype(BF16))).astype(BF16)
        return gs_ref[2:3, :] * _dot(hid, wd_ref[...].astype(BF16))

    @pl.when(j == 0)
    def _():
        o_ref[...] = x_ref[...] + gated_down_proj()

    @pl.when(j > 0)
    def _():
        o_ref[...] += gated_down_proj()

    if final:
        @pl.when(j == pl.num_programs(1) - 1)
        def _():
            _fill_rms_scale(tm, lambda rows: o_ref[rows, :], r_ref)

            def final_norm(rows):
                r = r_ref[rows, :]
                for cs in lane_tiles:
                    o_ref[rows, cs] = o_ref[rows, cs] * r * fg_ref[:, cs]

            _row_loop(tm, NORM_ROWS, final_norm, unroll=2)


def _ffn_call(x, mods, rows_per_cond, g_ffn, layer, wg, wu, wd, final_g=None, tm=1024, tn=256):
    r = x.shape[0]
    final = final_g is not None
    const = lambda i, j: (0, 0)
    row = lambda i, j: (i, 0)
    in_specs = [
        pl.BlockSpec((tm, D_MODEL), row),
        pl.BlockSpec((1, N_MODS, D_MODEL), lambda i, j: ((i * tm) // rows_per_cond, 0, 0)),
        pl.BlockSpec((1, D_MODEL), const),
        pl.BlockSpec((None, D_MODEL, tn), lambda i, j: (layer, 0, j)),
        pl.BlockSpec((None, D_MODEL, tn), lambda i, j: (layer, 0, j)),
        pl.BlockSpec((None, tn, D_MODEL), lambda i, j: (layer, j, 0)),
    ]
    args = [x, mods, g_ffn, wg, wu, wd]
    if final:
        in_specs.append(pl.BlockSpec((1, D_MODEL), const))
        args.append(final_g)
    return pl.pallas_call(
        functools.partial(_ffn_kernel, final),
        grid=(r // tm, D_FF // tn),
        in_specs=in_specs,
        out_specs=pl.BlockSpec((tm, D_MODEL), row),
        out_shape=jax.ShapeDtypeStruct((r, D_MODEL), F32),
        scratch_shapes=[
            pltpu.VMEM((tm, D_MODEL), BF16),
            pltpu.VMEM((SUBLANES, D_MODEL), F32),
            pltpu.VMEM((tm, LANES), F32),
        ],
        compiler_params=_cparams("ffn", 2),
        name="ffn",
    )(*args)


def _rope_tables(n_tokens):
    n_freq = ROPE_DIM // 4
    lane = np.arange(ROPE_DIM)
    tok = np.arange(n_tokens)
    pos = np.where(lane[None, :] // (2 * n_freq) == 0, tok[:, None] // GRID_W, tok[:, None] % GRID_W).astype(np.float32)
    exponent = -np.arange(n_freq, dtype=np.float32) * np.float32(2.0) / np.float32(ROPE_DIM // 2)
    inv_freq = np.power(np.float32(ROPE_THETA), exponent).astype(np.float32)
    ang = pos * inv_freq[lane % n_freq][None, :]
    sign = np.where((lane // n_freq) % 2 == 0, -1.0, 1.0).astype(np.float32)
    return np.cos(ang).astype(np.float32), (np.sin(ang) * sign[None, :]).astype(np.float32)


def _pair_swap(w):
    perm = jnp.arange(ROPE_DIM) ^ (ROPE_DIM // 4)
    return w[..., perm]


def kernel(x_prompt, x_sample, cache_ckv, cache_kpe, c, c_ctx, mod_w, mod_b, norm_mix_g, norm_ffn_g, ffn_w_gate, ffn_w_up, ffn_w_down, ev_w_in, ev_conv_w, ev_conv_b, ev_conv_ln_g, ev_conv_ln_b, ev_q_norm_g, ev_w_qb, ev_kv_norm_g, ev_w_kvb, ev_w_o, od_w_in, od_ln_g, od_ln_b, od_w_s, od_b_s, od_w_o, final_norm_g):
    bp, tp, _ = x_prompt.shape
    bs, ts, _ = x_sample.shape
    xp = x_prompt.reshape(bp * tp, D_MODEL)
    xs = x_sample.reshape(bs * ts, D_MODEL)

    cond8 = jnp.concatenate([c_ctx[None, :], c, jnp.zeros((N_MODS - 1 - bs, D_MODEL), F32)], axis=0)
    mods = _mods_call(cond8, mod_w, mod_b)
    mods = mods[:, :1 + bs].reshape(DEPTH, 1 + bs, 6, D_MODEL)
    mods = jnp.pad(mods, ((0, 0), (0, 0), (0, N_MODS - 6), (0, 0)))

    cos_k, sin_k = _rope_tables(ts)
    zpad = np.zeros((ts, LANES - ROPE_DIM), np.float32)
    cos_q = np.concatenate([cos_k, zpad], axis=-1)
    sin_q = np.concatenate([sin_k, zpad], axis=-1)

    streams = [
        dict(x=xp, batch=bp, t=tp, rows=bp * tp, conds=slice(0, 1), rope=False),
        dict(x=xs, batch=bs, t=ts, rows=ts, conds=slice(1, 1 + bs), rope=True),
    ]
    state = {}
    for l in range(DEPTH):
        g_mix = norm_mix_g[l][None, :]
        g_ffn = norm_ffn_g[l][None, :]
        last = l == DEPTH - 1
        if l % 2 == 0:
            e = l // 2
            w_in = ev_w_in[e]
            w1 = w_in.astype(BF16)
            wk2 = jnp.concatenate([w_in[:, O_KV:], _pair_swap(w_in[:, O_KV:])], axis=1).astype(BF16)
            wq3 = ev_w_qb[e].reshape(Q_RANK, MLA_HEADS, QK_DIM)
            wq = jnp.pad(wq3, ((0, 0), (0, 0), (0, HEAD_PAD - QK_DIM))).reshape(Q_RANK, -1).astype(BF16)
            wqs = jnp.pad(_pair_swap(wq3[:, :, NOPE_DIM:]), ((0, 0), (0, 0), (0, LANES - ROPE_DIM)))
            wqs = wqs.reshape(Q_RANK, -1).astype(BF16)
            wkv = ev_w_kvb[e].astype(BF16)
            wkv_t = ev_w_kvb[e].T.astype(BF16)
            w_o = ev_w_o[e].astype(BF16)
            conv_w = ev_conv_w[e].reshape(CONV_WIDTH, D_CONV)
            for s in streams:
                m = mods[l, s["conds"]]
                rope_args = (wqs, cos_q, sin_q, cos_k, sin_k) if s["rope"] else None
                a, q, ckv, kpe = _even_in_call(s["x"], m, s["rows"], g_mix, w1, wk2, ev_q_norm_g[e][None, :],
                                               ev_kv_norm_g[e][None, :], wq, rope_args)
                a = _conv_call(a, s["batch"], s["t"], conv_w, ev_conv_b[e][None, :],
                               ev_conv_ln_g[e][None, :], ev_conv_ln_b[e][None, :], tt=min(s["t"], 256))
                if s["rope"]:
                    att = _attn_call(q, ckv, kpe, wkv, wkv_t, s["batch"], s["t"],
                                     (cache_ckv[:, e], cache_kpe[:, e]), tq=2048)
                else:
                    att = _attn_seq_call(q, ckv, kpe, wkv, s["batch"], s["t"])
                    state.setdefault("ckv", []).append(ckv.reshape(bp, tp, KV_RANK))
                    state.setdefault("kpe", []).append(kpe.reshape(bp, tp, ROPE_DIM))
                s["x"] = _out_proj_call(s["x"], m, s["rows"], [a, att], w_o)
        else:
            o = l // 2
            w_in = od_w_in[o].astype(BF16)
            w_o = od_w_o[o].astype(BF16)
            w_s = od_w_s[o].astype(BF16)
            b_s_full = jnp.repeat(od_b_s[o].T, CMLP_GROUP_W, axis=1)
            for s in streams:
                m = mods[l, s["conds"]]
                p = _odd_in_call(s["x"], m, s["rows"], g_mix, w_in, od_ln_g[o][None, :], od_ln_b[o][None, :],
                                 w_s, b_s_full)
                s["x"] = _out_proj_call(s["x"], m, s["rows"], [p], w_o)
        for s in streams:
            m = mods[l, s["conds"]]
            s["x"] = _ffn_call(s["x"], m, s["rows"], g_ffn, l, ffn_w_gate, ffn_w_up, ffn_w_down,
                               final_g=final_norm_g[None, :] if last else None)

    y_prompt = streams[0]["x"].reshape(bp, tp, D_MODEL)
    y_sample = streams[1]["x"].reshape(bs, ts, D_MODEL)
    state_ckv = jnp.stack(state["ckv"], axis=1)
    state_kpe = jnp.stack(state["kpe"], axis=1)
    return (y_prompt, y_sample, state_ckv, state_kpe)
```

```python
import functools

import jax
import jax.numpy as jnp
import numpy as np
from jax import lax
from jax.experimental import pallas as pl
from jax.experimental.pallas import tpu as pltpu

D_MODEL = 2048
DEPTH = 2
GRID_W = 64
D_CONV = D_MODEL // 2
CONV_WIDTH = 31
CONV_PAD = CONV_WIDTH // 2
MLA_HEADS = D_MODEL // 256
Q_RANK = D_MODEL // 4
KV_RANK = D_MODEL // 8
NOPE_DIM = 128
ROPE_DIM = 64
V_DIM = 128
QK_DIM = NOPE_DIM + ROPE_DIM
ROPE_THETA = 10000.0
D_CMLP = D_MODEL
CHUNK = 128
CMLP_GROUPS = 8
CMLP_GROUP_W = D_CMLP // CMLP_GROUPS
D_FF = 256 * (-(-8 * D_MODEL // (3 * 256)))
ATTN_SCALE = QK_DIM ** -0.5
EXP2_SCALE = ATTN_SCALE * 1.4426950408889634
EPS = 1e-6

LANES = 128
SUBLANES = 8
HEAD_PAD = 2 * LANES
CONV_HALO = 2 * SUBLANES
N_MODS = 8
MIB = 1024 * 1024

BF16 = jnp.bfloat16
F32 = jnp.float32


VMEM_LIMIT_MIB = dict(mods=32, even_in=56, conv=32, attention=56, attention_seq=32, out_proj=48, odd_in=56, ffn=60)


def _cparams(name, grid_rank):
    return pltpu.CompilerParams(dimension_semantics=("arbitrary",) * grid_rank,
                                vmem_limit_bytes=VMEM_LIMIT_MIB[name] * MIB)


def _silu(x):
    return x * (1.0 / (1.0 + jnp.exp(-x)))


def _gelu_tanh(x):
    cdf = 0.5 * (1.0 + jnp.tanh(0.7978845608028654 * (x + 0.044715 * (x * x * x))))
    return x * cdf


def _rms(x, g):
    return (x * lax.rsqrt(jnp.mean(x * x, axis=-1, keepdims=True) + EPS)) * g


def _layernorm(x, g, b):
    mu = jnp.mean(x, axis=-1, keepdims=True)
    xc = x - mu
    var = jnp.mean(xc * xc, axis=-1, keepdims=True)
    return xc * lax.rsqrt(var + EPS) * g + b


def _modnorm(x, g, mods, row):
    return _rms(x, g) * (1.0 + mods[row + 1:row + 2, :]) + mods[row:row + 1, :]


def _dot(a, b):
    return jnp.dot(a, b, preferred_element_type=F32)


NORM_ROWS = 2 * SUBLANES


def _row_loop(n_rows, block, body, unroll=1):
    def step(i, carry):
        body(pl.ds(pl.multiple_of(i * block, block), block))
        return carry

    lax.fori_loop(0, n_rows // block, step, 0, unroll=unroll)


def _fill_rms_scale(n_rows, read, r_ref):
    def body(rows):
        x = read(rows)
        ms = jnp.mean(x * x, axis=-1, keepdims=True)
        r_ref[rows, :] = jnp.broadcast_to(lax.rsqrt(ms + EPS), (SUBLANES, LANES))

    _row_loop(n_rows, SUBLANES, body, unroll=32)


def _mods_kernel(cond_ref, w_ref, b_ref, o_ref):
    s = _silu(cond_ref[...]).astype(BF16)
    o_ref[0] = _dot(s, w_ref[0].astype(BF16)) + b_ref[0]


def _mods_call(cond8, mod_w, mod_b):
    tn = 1024
    n = mod_w.shape[-1]
    return pl.pallas_call(
        _mods_kernel,
        grid=(DEPTH, n // tn),
        in_specs=[
            pl.BlockSpec((N_MODS, D_MODEL), lambda l, j: (0, 0)),
            pl.BlockSpec((1, D_MODEL, tn), lambda l, j: (l, 0, j)),
            pl.BlockSpec((1, 1, tn), lambda l, j: (l, 0, j)),
        ],
        out_specs=pl.BlockSpec((1, N_MODS, tn), lambda l, j: (l, 0, j)),
        out_shape=jax.ShapeDtypeStruct((DEPTH, N_MODS, n), F32),
        compiler_params=_cparams("mods", 2),
        name="mods",
    )(cond8, mod_w, mod_b.reshape(DEPTH, 1, n))


O_GLU = 2 * D_CONV
O_Q = O_GLU + Q_RANK
O_KV = O_Q + KV_RANK


def _even_in_kernel(rope, x_ref, mods_ref, g_ref, w_ref, wk_ref, gq_ref, gkv_ref, wq_ref, *rest):
    if rope:
        wqs_ref, cq_ref, sq_ref, ck_ref, sk_ref, a_ref, q_ref, ckv_ref, kpe_ref = rest
    else:
        a_ref, q_ref, ckv_ref, kpe_ref = rest
    h = _modnorm(x_ref[...], g_ref[...], mods_ref[0], 0).astype(BF16)
    lin = _dot(h, w_ref[:, :D_CONV])
    gate = _dot(h, w_ref[:, D_CONV:O_GLU])
    a_ref[...] = lin * (1.0 / (1.0 + jnp.exp(-gate)))
    qn = _rms(_dot(h, w_ref[:, O_GLU:O_Q]), gq_ref[...]).astype(BF16)
    ckv_ref[...] = _rms(_dot(h, w_ref[:, O_Q:O_KV]), gkv_ref[...])
    kp = _dot(h, wk_ref[...])
    q = _dot(qn, wq_ref[...]) * EXP2_SCALE
    if rope:
        kpe_ref[...] = kp[:, :ROPE_DIM] * ck_ref[...] + kp[:, ROPE_DIM:] * sk_ref[...]
        qs = _dot(qn, wqs_ref[...]) * EXP2_SCALE
        cq = cq_ref[...]
        sq = sq_ref[...]
        for hd in range(MLA_HEADS):
            c0 = hd * HEAD_PAD
            q_ref[:, c0:c0 + LANES] = q[:, c0:c0 + LANES].astype(BF16)
            q_ref[:, c0 + LANES:c0 + HEAD_PAD] = (
                q[:, c0 + LANES:c0 + HEAD_PAD] * cq + qs[:, hd * LANES:(hd + 1) * LANES] * sq).astype(BF16)
    else:
        kpe_ref[...] = kp[:, :ROPE_DIM]
        q_ref[...] = q.astype(BF16)


def _even_in_call(x, mods, rows_per_cond, g_mix, w1, wk2, g_qa, g_kva, wq, rope_args, tm=256):
    r = x.shape[0]
    rope = rope_args is not None
    const = lambda i: (0, 0)
    row = lambda i: (i, 0)
    in_specs = [
        pl.BlockSpec((tm, D_MODEL), row),
        pl.BlockSpec((1, N_MODS, D_MODEL), lambda i: ((i * tm) // rows_per_cond, 0, 0)),
        pl.BlockSpec((1, D_MODEL), const),
        pl.BlockSpec(w1.shape, const),
        pl.BlockSpec(wk2.shape, const),
        pl.BlockSpec((1, Q_RANK), const),
        pl.BlockSpec((1, KV_RANK), const),
        pl.BlockSpec(wq.shape, const),
    ]
    args = [x, mods, g_mix, w1, wk2, g_qa, g_kva, wq]
    if rope:
        wqs, cq, sq, ck, sk = rope_args
        t_blocks = cq.shape[0] // tm
        pos = lambda i: (i % t_blocks, 0)
        in_specs += [
            pl.BlockSpec(wqs.shape, const),
            pl.BlockSpec((tm, LANES), pos),
            pl.BlockSpec((tm, LANES), pos),
            pl.BlockSpec((tm, ROPE_DIM), pos),
            pl.BlockSpec((tm, ROPE_DIM), pos),
        ]
        args += [wqs, cq, sq, ck, sk]
    return pl.pallas_call(
        functools.partial(_even_in_kernel, rope),
        grid=(r // tm,),
        in_specs=in_specs,
        out_specs=[
            pl.BlockSpec((tm, D_CONV), row),
            pl.BlockSpec((tm, MLA_HEADS * HEAD_PAD), row),
            pl.BlockSpec((tm, KV_RANK), row),
            pl.BlockSpec((tm, ROPE_DIM), row),
        ],
        out_shape=[
            jax.ShapeDtypeStruct((r, D_CONV), F32),
            jax.ShapeDtypeStruct((r, MLA_HEADS * HEAD_PAD), BF16),
            jax.ShapeDtypeStruct((r, KV_RANK), F32),
            jax.ShapeDtypeStruct((r, ROPE_DIM), F32),
        ],
        compiler_params=_cparams("even_in", 1),
        name="even_in",
    )(*args)


CONV_ROWS = 64
SHIFT_ROWS = 40


def _conv_kernel(tt, n_t, cur_ref, prev_ref, next_ref, w_ref, b_ref, g_ref, beta_ref, o_ref, buf_ref, sh_ref, y_ref):
    i = pl.program_id(1)
    keep_prev = (i > 0).astype(F32)
    keep_next = (i < n_t - 1).astype(F32)
    buf_ref[0:CONV_HALO, :] = prev_ref[0] * keep_prev
    buf_ref[CONV_HALO:CONV_HALO + tt, :] = cur_ref[0]
    buf_ref[CONV_HALO + tt:, :] = next_ref[0] * keep_next
    lane_tiles = [slice(c * LANES, (c + 1) * LANES) for c in range(D_CONV // LANES)]

    def shift(rows):
        for cs in lane_tiles:
            win = buf_ref[pl.ds(rows.start, SHIFT_ROWS + SUBLANES), cs]
            for r in range(1, SUBLANES):
                sh_ref[r - 1, rows, cs] = win[r:r + SHIFT_ROWS, :]

    _row_loop(tt + 2 * CONV_HALO - SUBLANES, SHIFT_ROWS, shift)

    first = CONV_HALO - CONV_PAD
    n_win = CONV_ROWS + (first + CONV_WIDTH - 1) // SUBLANES * SUBLANES

    def taps(rows):
        win_rows = pl.ds(rows.start, n_win)
        for cs in lane_tiles:
            acc = b_ref[:, cs]
            for r in range(SUBLANES):
                win = buf_ref[win_rows, cs] if r == 0 else sh_ref[r - 1, win_rows, cs]
                for q in range(n_win // SUBLANES):
                    k = q * SUBLANES + r - first
                    if 0 <= k < CONV_WIDTH:
                        acc = acc + win[q * SUBLANES:q * SUBLANES + CONV_ROWS, :] * w_ref[k:k + 1, cs]
            y_ref[rows, cs] = acc

    _row_loop(tt, CONV_ROWS, taps)
    o_ref[0] = _silu(_layernorm(y_ref[...], g_ref[...], beta_ref[...])).astype(BF16)


def _conv_call(a, batch, t, conv_w, conv_b, ln_g, ln_b, tt):
    n_t = t // tt
    assert tt % CONV_ROWS == 0 and (tt + 2 * CONV_HALO - SUBLANES) % SHIFT_ROWS == 0
    a3 = a.reshape(batch, t, D_CONV)
    hb = tt // CONV_HALO
    n_hb = t // CONV_HALO
    const = lambda b, i: (0, 0)
    out = pl.pallas_call(
        functools.partial(_conv_kernel, tt, n_t),
        grid=(batch, n_t),
        in_specs=[
            pl.BlockSpec((1, tt, D_CONV), lambda b, i: (b, i, 0)),
            pl.BlockSpec((1, CONV_HALO, D_CONV), lambda b, i: (b, jnp.maximum(i * hb - 1, 0), 0)),
            pl.BlockSpec((1, CONV_HALO, D_CONV), lambda b, i: (b, jnp.minimum((i + 1) * hb, n_hb - 1), 0)),
            pl.BlockSpec((CONV_WIDTH, D_CONV), const),
            pl.BlockSpec((1, D_CONV), const),
            pl.BlockSpec((1, D_CONV), const),
            pl.BlockSpec((1, D_CONV), const),
        ],
        out_specs=pl.BlockSpec((1, tt, D_CONV), lambda b, i: (b, i, 0)),
        out_shape=jax.ShapeDtypeStruct((batch, t, D_CONV), BF16),
        scratch_shapes=[
            pltpu.VMEM((tt + 2 * CONV_HALO, D_CONV), F32),
            pltpu.VMEM((SUBLANES - 1, tt + 2 * CONV_HALO - SUBLANES, D_CONV), F32),
            pltpu.VMEM((tt, D_CONV), F32),
        ],
        compiler_params=_cparams("conv", 2),
        name="conv",
    )(a3, a3, a3, conv_w, conv_b, ln_g, ln_b)
    return out.reshape(batch * t, D_CONV)


KV_ROWS = 1024
Q_TILE = 256
KEY_PIECE = 1024
NT_DIMS = (((1,), (1,)), ((), ()))


def _rope_key_pad(kpe):
    return jnp.concatenate([kpe, jnp.zeros((kpe.shape[0], HEAD_PAD - QK_DIM), F32)], axis=-1).astype(BF16)


def _fill_kv(k_ref, vt_ref, row0, ckv, kpe, wk, wvt):
    c = ckv.astype(BF16)
    n = c.shape[0]
    k_ref[row0:row0 + n, 0:NOPE_DIM] = _dot(c, wk).astype(BF16)
    k_ref[row0:row0 + n, NOPE_DIM:HEAD_PAD] = _rope_key_pad(kpe)
    vt_ref[:, row0:row0 + n] = lax.dot_general(wvt, c, NT_DIMS, preferred_element_type=F32).astype(BF16)


def _attn_kernel(t, past, q_ref, ckv_ref, kpe_ref, wk_ref, wvt_ref, cckv_ref, ckpe_ref, o_ref, k_ref, vt_ref, s_ref):
    @pl.when(pl.program_id(2) == 0)
    def _():
        wk = wk_ref[...]
        wvt = wvt_ref[...]
        _fill_kv(k_ref, vt_ref, 0, cckv_ref[0], ckpe_ref[0], wk, wvt)
        for r0 in range(0, t, KV_ROWS):
            _fill_kv(k_ref, vt_ref, past + r0, ckv_ref[r0:r0 + KV_ROWS, :], kpe_ref[r0:r0 + KV_ROWS, :], wk, wvt)

    n_tiles = q_ref.shape[0] // Q_TILE
    tk = k_ref.shape[0]
    pieces = [slice(k0, min(k0 + KEY_PIECE, tk)) for k0 in range(0, tk, KEY_PIECE)]

    def add(a, b):
        return b if a is None else a + b

    def score_piece(j, p, mx):
        q = q_ref[j * Q_TILE:(j + 1) * Q_TILE, :]
        s_ref[j % 2, pieces[p], :] = lax.dot_general(
            k_ref[pieces[p], :], q, NT_DIMS, preferred_element_type=F32)
        pm = jnp.max(s_ref[j % 2, pieces[p], :], axis=0, keepdims=True)
        return pm if mx is None else jnp.maximum(mx, pm)

    def exp_piece(j, p, m):
        e = jnp.exp2(s_ref[j % 2, pieces[p], :] - m)
        return e.astype(BF16), jnp.sum(e, axis=0, keepdims=True)

    def value_piece(p, e):
        return _dot(vt_ref[:, pieces[p]], e)

    m_next = None
    for p in range(len(pieces)):
        m_next = score_piece(0, p, m_next)
    for j in range(n_tiles):
        m, m_next = m_next, None
        acc = l = e_prev = None
        for p in range(len(pieces)):
            if j + 1 < n_tiles:
                m_next = score_piece(j + 1, p, m_next)
            if e_prev is not None:
                acc = add(acc, value_piece(p - 1, e_prev))
            e_prev, ls = exp_piece(j, p, m)
            l = add(l, ls)
        acc = add(acc, value_piece(len(pieces) - 1, e_prev))
        o_ref[j * Q_TILE:(j + 1) * Q_TILE, :] = (acc * (1.0 / l)).T.astype(BF16)


def _attn_call(q, ckv, kpe, wkv, wkv_t, batch, t, cache, tq):
    nq = t // tq
    past = cache[0].shape[1]
    tk = past + t
    assert t % KV_ROWS == 0 and past % LANES == 0 and tq % Q_TILE == 0
    return pl.pallas_call(
        functools.partial(_attn_kernel, t, past),
        grid=(batch, MLA_HEADS, nq),
        in_specs=[
            pl.BlockSpec((tq, HEAD_PAD), lambda b, h, i: (b * nq + i, h)),
            pl.BlockSpec((t, KV_RANK), lambda b, h, i: (b, 0)),
            pl.BlockSpec((t, ROPE_DIM), lambda b, h, i: (b, 0)),
            pl.BlockSpec((KV_RANK, NOPE_DIM), lambda b, h, i: (0, 2 * h)),
            pl.BlockSpec((V_DIM, KV_RANK), lambda b, h, i: (2 * h + 1, 0)),
            pl.BlockSpec((1, past, KV_RANK), lambda b, h, i: (b, 0, 0)),
            pl.BlockSpec((1, past, ROPE_DIM), lambda b, h, i: (b, 0, 0)),
        ],
        out_specs=pl.BlockSpec((tq, V_DIM), lambda b, h, i: (b * nq + i, h)),
        out_shape=jax.ShapeDtypeStruct((batch * t, MLA_HEADS * V_DIM), BF16),
        scratch_shapes=[
            pltpu.VMEM((tk, HEAD_PAD), BF16),
            pltpu.VMEM((V_DIM, tk), BF16),
            pltpu.VMEM((2, tk, Q_TILE), F32),
        ],
        compiler_params=_cparams("attention", 3),
        name="attention",
    )(q, ckv, kpe, wkv, wkv_t, *cache)


def _attn_seq_kernel(q_ref, ckv_ref, kpe_ref, wkv_ref, o_ref):
    kv = _dot(ckv_ref[...].astype(BF16), wkv_ref[...])
    kpad = _rope_key_pad(kpe_ref[...])
    for hd in range(MLA_HEADS):
        c0 = hd * (NOPE_DIM + V_DIM)
        k = jnp.concatenate([kv[:, c0:c0 + NOPE_DIM].astype(BF16), kpad], axis=-1)
        s = lax.dot_general(q_ref[:, hd * HEAD_PAD:(hd + 1) * HEAD_PAD], k, NT_DIMS, preferred_element_type=F32)
        m = jnp.max(s, axis=-1, keepdims=True)
        e = jnp.exp2(s - m)
        l = jnp.sum(e, axis=-1, keepdims=True)
        o = _dot(e.astype(BF16), kv[:, c0 + NOPE_DIM:c0 + NOPE_DIM + V_DIM].astype(BF16)) * (1.0 / l)
        o_ref[:, hd * V_DIM:(hd + 1) * V_DIM] = o.astype(BF16)


def _attn_seq_call(q, ckv, kpe, wkv, batch, t):
    row = lambda b: (b, 0)
    return pl.pallas_call(
        _attn_seq_kernel,
        grid=(batch,),
        in_specs=[
            pl.BlockSpec((t, MLA_HEADS * HEAD_PAD), row),
            pl.BlockSpec((t, KV_RANK), row),
            pl.BlockSpec((t, ROPE_DIM), row),
            pl.BlockSpec(wkv.shape, lambda b: (0, 0)),
        ],
        out_specs=pl.BlockSpec((t, MLA_HEADS * V_DIM), row),
        out_shape=jax.ShapeDtypeStruct((batch * t, MLA_HEADS * V_DIM), BF16),
        compiler_params=_cparams("attention_seq", 1),
        name="attention_seq",
    )(q, ckv, kpe, wkv)


def _out_proj_kernel(n_in, x_ref, mods_ref, *rest):
    in_refs = rest[:n_in]
    w_ref, o_ref = rest[n_in:]
    acc = None
    k0 = 0
    for r in in_refs:
        k = r.shape[-1]
        part = _dot(r[...], w_ref[k0:k0 + k, :])
        acc = part if acc is None else acc + part
        k0 += k
    o_ref[...] = x_ref[...] + mods_ref[0][2:3, :] * acc


def _out_proj_call(x, mods, rows_per_cond, parts, w_o, tm=512):
    r = x.shape[0]
    row = lambda i: (i, 0)
    in_specs = [
        pl.BlockSpec((tm, D_MODEL), row),
        pl.BlockSpec((1, N_MODS, D_MODEL), lambda i: ((i * tm) // rows_per_cond, 0, 0)),
    ]
    in_specs += [pl.BlockSpec((tm, p.shape[-1]), row) for p in parts]
    in_specs += [pl.BlockSpec(w_o.shape, lambda i: (0, 0))]
    return pl.pallas_call(
        functools.partial(_out_proj_kernel, len(parts)),
        grid=(r // tm,),
        in_specs=in_specs,
        out_specs=pl.BlockSpec((tm, D_MODEL), row),
        out_shape=jax.ShapeDtypeStruct((r, D_MODEL), F32),
        compiler_params=_cparams("out_proj", 1),
        name="out_proj",
    )(x, mods, *parts, w_o)


def _odd_in_kernel(tm, x_ref, mods_ref, g_ref, w_ref, lg_ref, lb_ref, ws_ref, bs_ref, o_ref):
    h = _modnorm(x_ref[...], g_ref[...], mods_ref[0], 0).astype(BF16)
    v = _gelu_tanh(_dot(h, w_ref[:, D_CMLP:]))
    v = _layernorm(v, lg_ref[...], lb_ref[...]).astype(BF16)
    u = _gelu_tanh(_dot(h, w_ref[:, :D_CMLP]))
    for ch in range(tm // CHUNK):
        rs = slice(ch * CHUNK, (ch + 1) * CHUNK)
        for g in range(CMLP_GROUPS):
            cs = slice(g * CMLP_GROUP_W, (g + 1) * CMLP_GROUP_W)
            mixed = _dot(ws_ref[g], v[rs, cs]) + bs_ref[:, cs]
            o_ref[rs, cs] = (u[rs, cs] * mixed).astype(BF16)


def _odd_in_call(x, mods, rows_per_cond, g_mix, w_in, ln_g, ln_b, w_s, b_s_full, tm=256):
    r = x.shape[0]
    const = lambda i: (0, 0)
    row = lambda i: (i, 0)
    return pl.pallas_call(
        functools.partial(_odd_in_kernel, tm),
        grid=(r // tm,),
        in_specs=[
            pl.BlockSpec((tm, D_MODEL), row),
            pl.BlockSpec((1, N_MODS, D_MODEL), lambda i: ((i * tm) // rows_per_cond, 0, 0)),
            pl.BlockSpec((1, D_MODEL), const),
            pl.BlockSpec(w_in.shape, const, pipeline_mode=pl.Buffered(1)),
            pl.BlockSpec((1, D_CMLP), const),
            pl.BlockSpec((1, D_CMLP), const),
            pl.BlockSpec(w_s.shape, lambda i: (0, 0, 0)),
            pl.BlockSpec((CHUNK, D_CMLP), const),
        ],
        out_specs=pl.BlockSpec((tm, D_CMLP), row),
        out_shape=jax.ShapeDtypeStruct((r, D_CMLP), BF16),
        compiler_params=_cparams("odd_in", 1),
        name="odd_in",
    )(x, mods, g_mix, w_in, ln_g, ln_b, w_s, b_s_full)


FFN_NEXT_ROWS = 64


def _ffn_kernel(final, x_ref, xn_ref, mods_ref, modsn_ref, g_ref, wg_ref, wu_ref, wd_ref, *rest):
    if final:
        fg_ref, o_ref, h_ref, gs_ref, r_ref = rest
    else:
        o_ref, h_ref, gs_ref, r_ref = rest
    i = pl.program_id(0)
    j = pl.program_id(1)
    tm = x_ref.shape[0]
    cur = i % 2
    lane_tiles = [slice(c * LANES, (c + 1) * LANES) for c in range(D_MODEL // LANES)]

    @pl.when(jnp.logical_and(i == 0, j == 0))
    def _():
        gs_ref[0:1, :] = g_ref[...] * (1.0 + mods_ref[0][4:5, :])
        gs_ref[1:2, :] = mods_ref[0][3:4, :]
        _fill_rms_scale(tm, lambda rows: x_ref[rows, :], r_ref)

        def prologue(rows):
            r = r_ref[rows, :]
            for cs in lane_tiles:
                h_ref[0, rows, cs] = (x_ref[rows, cs] * r * gs_ref[0:1, cs] + gs_ref[1:2, cs]).astype(BF16)

        _row_loop(tm, NORM_ROWS, prologue, unroll=2)

    @pl.when(j == 0)
    def _():
        gs_ref[2:3, :] = mods_ref[0][5:6, :]

    def gated_down_proj():
        h = h_ref[cur]
        gate = _dot(h, wg_ref[...].astype(BF16))
        xs = xn_ref[...]
        rs = lax.rsqrt(jnp.mean(xs * xs, axis=-1, keepdims=True) + EPS)
        h_next = (xs * rs * (g_ref[...] * (1.0 + modsn_ref[0][4:5, :])) + modsn_ref[0][3:4, :]).astype(BF16)
        hid = (_silu(gate) * _dot(h, wu_ref[...].astype(BF16))).astype(BF16)
        out = gs_ref[2:3, :] * _dot(hid, wd_ref[...].astype(BF16))
        row0 = pl.multiple_of(jnp.minimum(j, tm // FFN_NEXT_ROWS - 1) * FFN_NEXT_ROWS, FFN_NEXT_ROWS)
        h_ref[1 - cur, pl.ds(row0, FFN_NEXT_ROWS), :] = h_next
        return out

    @pl.when(j == 0)
    def _():
        o_ref[...] = x_ref[...] + gated_down_proj()

    @pl.when(j > 0)
    def _():
        o_ref[...] += gated_down_proj()

    if final:
        @pl.when(j == pl.num_programs(1) - 1)
        def _():
            _fill_rms_scale(tm, lambda rows: o_ref[rows, :], r_ref)

            def final_norm(rows):
                r = r_ref[rows, :]
                for cs in lane_tiles:
                    o_ref[rows, cs] = o_ref[rows, cs] * r * fg_ref[:, cs]

            _row_loop(tm, NORM_ROWS, final_norm, unroll=2)


def _ffn_call(x, mods, rows_per_cond, g_ffn, layer, wg, wu, wd, final_g=None, tm=1024, tn=256):
    r = x.shape[0]
    final = final_g is not None
    const = lambda i, j: (0, 0)
    row = lambda i, j: (i, 0)
    n_i = r // tm
    n_next = tm // FFN_NEXT_ROWS
    assert D_FF // tn >= n_next
    nxt = lambda i: jnp.minimum(i + 1, n_i - 1)
    in_specs = [
        pl.BlockSpec((tm, D_MODEL), row),
        pl.BlockSpec((FFN_NEXT_ROWS, D_MODEL), lambda i, j: (nxt(i) * n_next + jnp.minimum(j, n_next - 1), 0)),
        pl.BlockSpec((1, N_MODS, D_MODEL), lambda i, j: ((i * tm) // rows_per_cond, 0, 0)),
        pl.BlockSpec((1, N_MODS, D_MODEL), lambda i, j: ((nxt(i) * tm) // rows_per_cond, 0, 0)),
        pl.BlockSpec((1, D_MODEL), const),
        pl.BlockSpec((None, D_MODEL, tn), lambda i, j: (layer, 0, j)),
        pl.BlockSpec((None, D_MODEL, tn), lambda i, j: (layer, 0, j)),
        pl.BlockSpec((None, tn, D_MODEL), lambda i, j: (layer, j, 0)),
    ]
    args = [x, x, mods, mods, g_ffn, wg, wu, wd]
    if final:
        in_specs.append(pl.BlockSpec((1, D_MODEL), const))
        args.append(final_g)
    return pl.pallas_call(
        functools.partial(_ffn_kernel, final),
        grid=(r // tm, D_FF // tn),
        in_specs=in_specs,
        out_specs=pl.BlockSpec((tm, D_MODEL), row),
        out_shape=jax.ShapeDtypeStruct((r, D_MODEL), F32),
        scratch_shapes=[
            pltpu.VMEM((2, tm, D_MODEL), BF16),
            pltpu.VMEM((SUBLANES, D_MODEL), F32),
            pltpu.VMEM((tm, LANES), F32),
        ],
        compiler_params=_cparams("ffn", 2),
        name="ffn",
    )(*args)


def _rope_tables(n_tokens):
    n_freq = ROPE_DIM // 4
    lane = np.arange(ROPE_DIM)
    tok = np.arange(n_tokens)
    pos = np.where(lane[None, :] // (2 * n_freq) == 0, tok[:, None] // GRID_W, tok[:, None] % GRID_W).astype(np.float32)
    exponent = -np.arange(n_freq, dtype=np.float32) * np.float32(2.0) / np.float32(ROPE_DIM // 2)
    inv_freq = np.power(np.float32(ROPE_THETA), exponent).astype(np.float32)
    ang = pos * inv_freq[lane % n_freq][None, :]
    sign = np.where((lane // n_freq) % 2 == 0, -1.0, 1.0).astype(np.float32)
    return np.cos(ang).astype(np.float32), (np.sin(ang) * sign[None, :]).astype(np.float32)


def _pair_swap(w):
    perm = jnp.arange(ROPE_DIM) ^ (ROPE_DIM // 4)
    return w[..., perm]


def kernel(x_prompt, x_sample, cache_ckv, cache_kpe, c, c_ctx, mod_w, mod_b, norm_mix_g, norm_ffn_g, ffn_w_gate, ffn_w_up, ffn_w_down, ev_w_in, ev_conv_w, ev_conv_b, ev_conv_ln_g, ev_conv_ln_b, ev_q_norm_g, ev_w_qb, ev_kv_norm_g, ev_w_kvb, ev_w_o, od_w_in, od_ln_g, od_ln_b, od_w_s, od_b_s, od_w_o, final_norm_g):
    bp, tp, _ = x_prompt.shape
    bs, ts, _ = x_sample.shape
    xp = x_prompt.reshape(bp * tp, D_MODEL)
    xs = x_sample.reshape(bs * ts, D_MODEL)

    cond8 = jnp.concatenate([c_ctx[None, :], c, jnp.zeros((N_MODS - 1 - bs, D_MODEL), F32)], axis=0)
    mods = _mods_call(cond8, mod_w, mod_b)
    mods = mods[:, :1 + bs].reshape(DEPTH, 1 + bs, 6, D_MODEL)
    mods = jnp.pad(mods, ((0, 0), (0, 0), (0, N_MODS - 6), (0, 0)))

    cos_k, sin_k = _rope_tables(ts)
    zpad = np.zeros((ts, LANES - ROPE_DIM), np.float32)
    cos_q = np.concatenate([cos_k, zpad], axis=-1)
    sin_q = np.concatenate([sin_k, zpad], axis=-1)

    streams = [
        dict(x=xp, batch=bp, t=tp, rows=bp * tp, conds=slice(0, 1), rope=False),
        dict(x=xs, batch=bs, t=ts, rows=ts, conds=slice(1, 1 + bs), rope=True),
    ]
    state = {}
    for l in range(DEPTH):
        g_mix = norm_mix_g[l][None, :]
        g_ffn = norm_ffn_g[l][None, :]
        last = l == DEPTH - 1
        if l % 2 == 0:
            e = l // 2
            w_in = ev_w_in[e]
            w1 = w_in.astype(BF16)
            wk2 = jnp.concatenate([w_in[:, O_KV:], _pair_swap(w_in[:, O_KV:])], axis=1).astype(BF16)
            wq3 = ev_w_qb[e].reshape(Q_RANK, MLA_HEADS, QK_DIM)
            wq = jnp.pad(wq3, ((0, 0), (0, 0), (0, HEAD_PAD - QK_DIM))).reshape(Q_RANK, -1).astype(BF16)
            wqs = jnp.pad(_pair_swap(wq3[:, :, NOPE_DIM:]), ((0, 0), (0, 0), (0, LANES - ROPE_DIM)))
            wqs = wqs.reshape(Q_RANK, -1).astype(BF16)
            wkv = ev_w_kvb[e].astype(BF16)
            wkv_t = ev_w_kvb[e].T.astype(BF16)
            w_o = ev_w_o[e].astype(BF16)
            conv_w = ev_conv_w[e].reshape(CONV_WIDTH, D_CONV)
            for s in streams:
                m = mods[l, s["conds"]]
                rope_args = (wqs, cos_q, sin_q, cos_k, sin_k) if s["rope"] else None
                a, q, ckv, kpe = _even_in_call(s["x"], m, s["rows"], g_mix, w1, wk2, ev_q_norm_g[e][None, :],
                                               ev_kv_norm_g[e][None, :], wq, rope_args)
                a = _conv_call(a, s["batch"], s["t"], conv_w, ev_conv_b[e][None, :],
                               ev_conv_ln_g[e][None, :], ev_conv_ln_b[e][None, :], tt=min(s["t"], 256))
                if s["rope"]:
                    att = _attn_call(q, ckv, kpe, wkv, wkv_t, s["batch"], s["t"],
                                     (cache_ckv[:, e], cache_kpe[:, e]), tq=2048)
                else:
                    att = _attn_seq_call(q, ckv, kpe, wkv, s["batch"], s["t"])
                    state.setdefault("ckv", []).append(ckv.reshape(bp, tp, KV_RANK))
                    state.setdefault("kpe", []).append(kpe.reshape(bp, tp, ROPE_DIM))
                s["x"] = _out_proj_call(s["x"], m, s["rows"], [a, att], w_o)
        else:
            o = l // 2
            w_in = od_w_in[o].astype(BF16)
            w_o = od_w_o[o].astype(BF16)
            w_s = od_w_s[o].astype(BF16)
            b_s_full = jnp.repeat(od_b_s[o].T, CMLP_GROUP_W, axis=1)
            for s in streams:
                m = mods[l, s["conds"]]
                p = _odd_in_call(s["x"], m, s["rows"], g_mix, w_in, od_ln_g[o][None, :], od_ln_b[o][None, :],
                                 w_s, b_s_full)
                s["x"] = _out_proj_call(s["x"], m, s["rows"], [p], w_o)
        for s in streams:
            m = mods[l, s["conds"]]
            s["x"] = _ffn_call(s["x"], m, s["rows"], g_ffn, l, ffn_w_gate, ffn_w_up, ffn_w_down,
                               final_g=final_norm_g[None, :] if last else None)

    y_prompt = streams[0]["x"].reshape(bp, tp, D_MODEL)
    y_sample = streams[1]["x"].reshape(bs, ts, D_MODEL)
    state_ckv = jnp.stack(state["ckv"], axis=1)
    state_kpe = jnp.stack(state["kpe"], axis=1)
    return (y_prompt, y_sample, state_ckv, state_kpe)
```

```python
import functools

import jax
import jax.numpy as jnp
import numpy as np
from jax import lax
from jax.experimental import pallas as pl
from jax.experimental.pallas import tpu as pltpu

D_MODEL = 2048
DEPTH = 2
GRID_W = 64
D_CONV = D_MODEL // 2
CONV_WIDTH = 31
CONV_PAD = CONV_WIDTH // 2
MLA_HEADS = D_MODEL // 256
Q_RANK = D_MODEL // 4
KV_RANK = D_MODEL // 8
NOPE_DIM = 128
ROPE_DIM = 64
V_DIM = 128
QK_DIM = NOPE_DIM + ROPE_DIM
ROPE_THETA = 10000.0
D_CMLP = D_MODEL
CHUNK = 128
CMLP_GROUPS = 8
CMLP_GROUP_W = D_CMLP // CMLP_GROUPS
D_FF = 256 * (-(-8 * D_MODEL // (3 * 256)))
ATTN_SCALE = QK_DIM ** -0.5
EXP2_SCALE = ATTN_SCALE * 1.4426950408889634
EPS = 1e-6

LANES = 128
SUBLANES = 8
HEAD_PAD = 2 * LANES
CONV_HALO = 2 * SUBLANES
N_MODS = 8
MIB = 1024 * 1024

BF16 = jnp.bfloat16
F32 = jnp.float32


VMEM_LIMIT_MIB = dict(mods=32, even_in=56, conv=32, attention=56, attention_seq=32, out_proj=48, odd_in=56, ffn=60)


def _cparams(name, grid_rank):
    return pltpu.CompilerParams(dimension_semantics=("arbitrary",) * grid_rank,
                                vmem_limit_bytes=VMEM_LIMIT_MIB[name] * MIB)


def _silu(x):
    return x * (1.0 / (1.0 + jnp.exp(-x)))


def _gelu_tanh(x):
    cdf = 0.5 * (1.0 + jnp.tanh(0.7978845608028654 * (x + 0.044715 * (x * x * x))))
    return x * cdf


def _rms(x, g):
    return (x * lax.rsqrt(jnp.mean(x * x, axis=-1, keepdims=True) + EPS)) * g


def _layernorm(x, g, b):
    mu = jnp.mean(x, axis=-1, keepdims=True)
    xc = x - mu
    var = jnp.mean(xc * xc, axis=-1, keepdims=True)
    return xc * lax.rsqrt(var + EPS) * g + b


def _modnorm(x, g, mods, row):
    return _rms(x, g) * (1.0 + mods[row + 1:row + 2, :]) + mods[row:row + 1, :]


def _dot(a, b):
    return jnp.dot(a, b, preferred_element_type=F32)


NORM_ROWS = 2 * SUBLANES


def _row_loop(n_rows, block, body, unroll=1):
    def step(i, carry):
        body(pl.ds(pl.multiple_of(i * block, block), block))
        return carry

    lax.fori_loop(0, n_rows // block, step, 0, unroll=unroll)


def _fill_rms_scale(n_rows, read, r_ref):
    def body(rows):
        x = read(rows)
        ms = jnp.mean(x * x, axis=-1, keepdims=True)
        r_ref[rows, :] = jnp.broadcast_to(lax.rsqrt(ms + EPS), (SUBLANES, LANES))

    _row_loop(n_rows, SUBLANES, body, unroll=32)


def _mods_kernel(cond_ref, w_ref, b_ref, o_ref):
    s = _silu(cond_ref[...]).astype(BF16)
    o_ref[0] = _dot(s, w_ref[0].astype(BF16)) + b_ref[0]


def _mods_call(cond8, mod_w, mod_b):
    tn = 1024
    n = mod_w.shape[-1]
    return pl.pallas_call(
        _mods_kernel,
        grid=(DEPTH, n // tn),
        in_specs=[
            pl.BlockSpec((N_MODS, D_MODEL), lambda l, j: (0, 0)),
            pl.BlockSpec((1, D_MODEL, tn), lambda l, j: (l, 0, j)),
            pl.BlockSpec((1, 1, tn), lambda l, j: (l, 0, j)),
        ],
        out_specs=pl.BlockSpec((1, N_MODS, tn), lambda l, j: (l, 0, j)),
        out_shape=jax.ShapeDtypeStruct((DEPTH, N_MODS, n), F32),
        compiler_params=_cparams("mods", 2),
        name="mods",
    )(cond8, mod_w, mod_b.reshape(DEPTH, 1, n))


O_GLU = 2 * D_CONV
O_Q = O_GLU + Q_RANK
O_KV = O_Q + KV_RANK


def _even_in_kernel(rope, x_ref, mods_ref, g_ref, w_ref, wk_ref, gq_ref, gkv_ref, wq_ref, *rest):
    if rope:
        wqs_ref, cq_ref, sq_ref, ck_ref, sk_ref, a_ref, q_ref, ckv_ref, kpe_ref = rest
    else:
        a_ref, q_ref, ckv_ref, kpe_ref = rest
    h = _modnorm(x_ref[...], g_ref[...], mods_ref[0], 0).astype(BF16)
    lin = _dot(h, w_ref[:, :D_CONV])
    gate = _dot(h, w_ref[:, D_CONV:O_GLU])
    a_ref[...] = lin * (1.0 / (1.0 + jnp.exp(-gate)))
    qn = _rms(_dot(h, w_ref[:, O_GLU:O_Q]), gq_ref[...]).astype(BF16)
    ckv_ref[...] = _rms(_dot(h, w_ref[:, O_Q:O_KV]), gkv_ref[...])
    kp = _dot(h, wk_ref[...])
    q = _dot(qn, wq_ref[...]) * EXP2_SCALE
    if rope:
        kpe_ref[...] = kp[:, :ROPE_DIM] * ck_ref[...] + kp[:, ROPE_DIM:] * sk_ref[...]
        qs = _dot(qn, wqs_ref[...]) * EXP2_SCALE
        cq = cq_ref[...]
        sq = sq_ref[...]
        for hd in range(MLA_HEADS):
            c0 = hd * HEAD_PAD
            q_ref[:, c0:c0 + LANES] = q[:, c0:c0 + LANES].astype(BF16)
            q_ref[:, c0 + LANES:c0 + HEAD_PAD] = (
                q[:, c0 + LANES:c0 + HEAD_PAD] * cq + qs[:, hd * LANES:(hd + 1) * LANES] * sq).astype(BF16)
    else:
        kpe_ref[...] = kp[:, :ROPE_DIM]
        q_ref[...] = q.astype(BF16)


def _even_in_call(x, mods, rows_per_cond, g_mix, w1, wk2, g_qa, g_kva, wq, rope_args, tm=256):
    r = x.shape[0]
    rope = rope_args is not None
    const = lambda i: (0, 0)
    row = lambda i: (i, 0)
    in_specs = [
        pl.BlockSpec((tm, D_MODEL), row),
        pl.BlockSpec((1, N_MODS, D_MODEL), lambda i: ((i * tm) // rows_per_cond, 0, 0)),
        pl.BlockSpec((1, D_MODEL), const),
        pl.BlockSpec(w1.shape, const),
        pl.BlockSpec(wk2.shape, const),
        pl.BlockSpec((1, Q_RANK), const),
        pl.BlockSpec((1, KV_RANK), const),
        pl.BlockSpec(wq.shape, const),
    ]
    args = [x, mods, g_mix, w1, wk2, g_qa, g_kva, wq]
    if rope:
        wqs, cq, sq, ck, sk = rope_args
        t_blocks = cq.shape[0] // tm
        pos = lambda i: (i % t_blocks, 0)
        in_specs += [
            pl.BlockSpec(wqs.shape, const),
            pl.BlockSpec((tm, LANES), pos),
            pl.BlockSpec((tm, LANES), pos),
            pl.BlockSpec((tm, ROPE_DIM), pos),
            pl.BlockSpec((tm, ROPE_DIM), pos),
        ]
        args += [wqs, cq, sq, ck, sk]
    return pl.pallas_call(
        functools.partial(_even_in_kernel, rope),
        grid=(r // tm,),
        in_specs=in_specs,
        out_specs=[
            pl.BlockSpec((tm, D_CONV), row),
            pl.BlockSpec((tm, MLA_HEADS * HEAD_PAD), row),
            pl.BlockSpec((tm, KV_RANK), row),
            pl.BlockSpec((tm, ROPE_DIM), row),
        ],
        out_shape=[
            jax.ShapeDtypeStruct((r, D_CONV), F32),
            jax.ShapeDtypeStruct((r, MLA_HEADS * HEAD_PAD), BF16),
            jax.ShapeDtypeStruct((r, KV_RANK), F32),
            jax.ShapeDtypeStruct((r, ROPE_DIM), F32),
        ],
        compiler_params=_cparams("even_in", 1),
        name="even_in",
    )(*args)


CONV_ROWS = 64
SHIFT_ROWS = 40


def _conv_kernel(tt, n_t, cur_ref, prev_ref, next_ref, w_ref, b_ref, g_ref, beta_ref, o_ref, buf_ref, sh_ref, y_ref):
    i = pl.program_id(1)
    keep_prev = (i > 0).astype(F32)
    keep_next = (i < n_t - 1).astype(F32)
    buf_ref[0:CONV_HALO, :] = prev_ref[0] * keep_prev
    buf_ref[CONV_HALO:CONV_HALO + tt, :] = cur_ref[0]
    buf_ref[CONV_HALO + tt:, :] = next_ref[0] * keep_next
    lane_tiles = [slice(c * LANES, (c + 1) * LANES) for c in range(D_CONV // LANES)]

    def shift(rows):
        for cs in lane_tiles:
            win = buf_ref[pl.ds(rows.start, SHIFT_ROWS + SUBLANES), cs]
            for r in range(1, SUBLANES):
                sh_ref[r - 1, rows, cs] = win[r:r + SHIFT_ROWS, :]

    _row_loop(tt + 2 * CONV_HALO - SUBLANES, SHIFT_ROWS, shift)

    first = CONV_HALO - CONV_PAD
    n_win = CONV_ROWS + (first + CONV_WIDTH - 1) // SUBLANES * SUBLANES

    def taps(rows):
        win_rows = pl.ds(rows.start, n_win)
        for cs in lane_tiles:
            acc = b_ref[:, cs]
            for r in range(SUBLANES):
                win = buf_ref[win_rows, cs] if r == 0 else sh_ref[r - 1, win_rows, cs]
                for q in range(n_win // SUBLANES):
                    k = q * SUBLANES + r - first
                    if 0 <= k < CONV_WIDTH:
                        acc = acc + win[q * SUBLANES:q * SUBLANES + CONV_ROWS, :] * w_ref[k:k + 1, cs]
            y_ref[rows, cs] = acc

    _row_loop(tt, CONV_ROWS, taps)
    o_ref[0] = _silu(_layernorm(y_ref[...], g_ref[...], beta_ref[...])).astype(BF16)


def _conv_call(a, batch, t, conv_w, conv_b, ln_g, ln_b, tt):
    n_t = t // tt
    assert tt % CONV_ROWS == 0 and (tt + 2 * CONV_HALO - SUBLANES) % SHIFT_ROWS == 0
    a3 = a.reshape(batch, t, D_CONV)
    hb = tt // CONV_HALO
    n_hb = t // CONV_HALO
    const = lambda b, i: (0, 0)
    out = pl.pallas_call(
        functools.partial(_conv_kernel, tt, n_t),
        grid=(batch, n_t),
        in_specs=[
            pl.BlockSpec((1, tt, D_CONV), lambda b, i: (b, i, 0)),
            pl.BlockSpec((1, CONV_HALO, D_CONV), lambda b, i: (b, jnp.maximum(i * hb - 1, 0), 0)),
            pl.BlockSpec((1, CONV_HALO, D_CONV), lambda b, i: (b, jnp.minimum((i + 1) * hb, n_hb - 1), 0)),
            pl.BlockSpec((CONV_WIDTH, D_CONV), const),
            pl.BlockSpec((1, D_CONV), const),
            pl.BlockSpec((1, D_CONV), const),
            pl.BlockSpec((1, D_CONV), const),
        ],
        out_specs=pl.BlockSpec((1, tt, D_CONV), lambda b, i: (b, i, 0)),
        out_shape=jax.ShapeDtypeStruct((batch, t, D_CONV), BF16),
        scratch_shapes=[
            pltpu.VMEM((tt + 2 * CONV_HALO, D_CONV), F32),
            pltpu.VMEM((SUBLANES - 1, tt + 2 * CONV_HALO - SUBLANES, D_CONV), F32),
            pltpu.VMEM((tt, D_CONV), F32),
        ],
        compiler_params=_cparams("conv", 2),
        name="conv",
    )(a3, a3, a3, conv_w, conv_b, ln_g, ln_b)
    return out.reshape(batch * t, D_CONV)


KV_ROWS = 1024
Q_TILE = 256
KEY_PIECE = 1024
NT_DIMS = (((1,), (1,)), ((), ()))


def _rope_key_pad(kpe):
    return jnp.concatenate([kpe, jnp.zeros((kpe.shape[0], HEAD_PAD - QK_DIM), F32)], axis=-1).astype(BF16)


def _fill_kv(k_ref, vt_ref, row0, ckv, kpe, wk, wvt):
    c = ckv.astype(BF16)
    n = c.shape[0]
    k_ref[row0:row0 + n, 0:NOPE_DIM] = _dot(c, wk).astype(BF16)
    k_ref[row0:row0 + n, NOPE_DIM:HEAD_PAD] = _rope_key_pad(kpe)
    vt_ref[:, row0:row0 + n] = lax.dot_general(wvt, c, NT_DIMS, preferred_element_type=F32).astype(BF16)


def _attn_kernel(t, past, q_ref, ckv_ref, kpe_ref, wk_ref, wvt_ref, cckv_ref, ckpe_ref, o_ref, k_ref, vt_ref, s_ref):
    @pl.when(pl.program_id(2) == 0)
    def _():
        wk = wk_ref[...]
        wvt = wvt_ref[...]
        _fill_kv(k_ref, vt_ref, 0, cckv_ref[0], ckpe_ref[0], wk, wvt)
        for r0 in range(0, t, KV_ROWS):
            _fill_kv(k_ref, vt_ref, past + r0, ckv_ref[r0:r0 + KV_ROWS, :], kpe_ref[r0:r0 + KV_ROWS, :], wk, wvt)

    n_tiles = q_ref.shape[0] // Q_TILE
    tk = k_ref.shape[0]
    pieces = [slice(k0, min(k0 + KEY_PIECE, tk)) for k0 in range(0, tk, KEY_PIECE)]

    def add(a, b):
        return b if a is None else a + b

    def score_piece(j, p, mx):
        q = q_ref[j * Q_TILE:(j + 1) * Q_TILE, :]
        s_ref[j % 2, pieces[p], :] = lax.dot_general(
            k_ref[pieces[p], :], q, NT_DIMS, preferred_element_type=F32)
        pm = jnp.max(s_ref[j % 2, pieces[p], :], axis=0, keepdims=True)
        return pm if mx is None else jnp.maximum(mx, pm)

    def exp_piece(j, p, m):
        e = jnp.exp2(s_ref[j % 2, pieces[p], :] - m)
        return e.astype(BF16), jnp.sum(e, axis=0, keepdims=True)

    def value_piece(p, e):
        return _dot(vt_ref[:, pieces[p]], e)

    m_next = None
    for p in range(len(pieces)):
        m_next = score_piece(0, p, m_next)
    for j in range(n_tiles):
        m, m_next = m_next, None
        acc = l = e_prev = None
        for p in range(len(pieces)):
            if j + 1 < n_tiles:
                m_next = score_piece(j + 1, p, m_next)
            if e_prev is not None:
                acc = add(acc, value_piece(p - 1, e_prev))
            e_prev, ls = exp_piece(j, p, m)
            l = add(l, ls)
        acc = add(acc, value_piece(len(pieces) - 1, e_prev))
        o_ref[j * Q_TILE:(j + 1) * Q_TILE, :] = (acc * (1.0 / l)).T.astype(BF16)


def _attn_call(q, ckv, kpe, wkv, wkv_t, batch, t, cache, tq):
    nq = t // tq
    past = cache[0].shape[1]
    tk = past + t
    assert t % KV_ROWS == 0 and past % LANES == 0 and tq % Q_TILE == 0
    return pl.pallas_call(
        functools.partial(_attn_kernel, t, past),
        grid=(batch, MLA_HEADS, nq),
        in_specs=[
            pl.BlockSpec((tq, HEAD_PAD), lambda b, h, i: (b * nq + i, h)),
            pl.BlockSpec((t, KV_RANK), lambda b, h, i: (b, 0)),
            pl.BlockSpec((t, ROPE_DIM), lambda b, h, i: (b, 0)),
            pl.BlockSpec((KV_RANK, NOPE_DIM), lambda b, h, i: (0, 2 * h)),
            pl.BlockSpec((V_DIM, KV_RANK), lambda b, h, i: (2 * h + 1, 0)),
            pl.BlockSpec((1, past, KV_RANK), lambda b, h, i: (b, 0, 0)),
            pl.BlockSpec((1, past, ROPE_DIM), lambda b, h, i: (b, 0, 0)),
        ],
        out_specs=pl.BlockSpec((tq, V_DIM), lambda b, h, i: (b * nq + i, h)),
        out_shape=jax.ShapeDtypeStruct((batch * t, MLA_HEADS * V_DIM), BF16),
        scratch_shapes=[
            pltpu.VMEM((tk, HEAD_PAD), BF16),
            pltpu.VMEM((V_DIM, tk), BF16),
            pltpu.VMEM((2, tk, Q_TILE), F32),
        ],
        compiler_params=_cparams("attention", 3),
        name="attention",
    )(q, ckv, kpe, wkv, wkv_t, *cache)


SEQS_PER_STEP = 2


def _attn_seq_kernel(t, q_ref, ckv_ref, kpe_ref, wkv_ref, o_ref):
    kv = _dot(ckv_ref[...].astype(BF16), wkv_ref[...])
    kpad = _rope_key_pad(kpe_ref[...])
    for sq in range(SEQS_PER_STEP):
        rows = slice(sq * t, (sq + 1) * t)
        for hd in range(MLA_HEADS):
            c0 = hd * (NOPE_DIM + V_DIM)
            k = jnp.concatenate([kv[rows, c0:c0 + NOPE_DIM].astype(BF16), kpad[rows, :]], axis=-1)
            s = lax.dot_general(q_ref[rows, hd * HEAD_PAD:(hd + 1) * HEAD_PAD], k, NT_DIMS,
                                preferred_element_type=F32)
            m = jnp.max(s, axis=-1, keepdims=True)
            e = jnp.exp2(s - m)
            l = jnp.sum(e, axis=-1, keepdims=True)
            o = _dot(e.astype(BF16), kv[rows, c0 + NOPE_DIM:c0 + NOPE_DIM + V_DIM].astype(BF16)) * (1.0 / l)
            o_ref[rows, hd * V_DIM:(hd + 1) * V_DIM] = o.astype(BF16)


def _attn_seq_call(q, ckv, kpe, wkv, batch, t):
    assert batch % SEQS_PER_STEP == 0
    row = lambda b: (b, 0)
    rows = SEQS_PER_STEP * t
    return pl.pallas_call(
        functools.partial(_attn_seq_kernel, t),
        grid=(batch // SEQS_PER_STEP,),
        in_specs=[
            pl.BlockSpec((rows, MLA_HEADS * HEAD_PAD), row),
            pl.BlockSpec((rows, KV_RANK), row),
            pl.BlockSpec((rows, ROPE_DIM), row),
            pl.BlockSpec(wkv.shape, lambda b: (0, 0)),
        ],
        out_specs=pl.BlockSpec((rows, MLA_HEADS * V_DIM), row),
        out_shape=jax.ShapeDtypeStruct((batch * t, MLA_HEADS * V_DIM), BF16),
        compiler_params=_cparams("attention_seq", 1),
        name="attention_seq",
    )(q, ckv, kpe, wkv)


def _out_proj_kernel(n_in, x_ref, mods_ref, *rest):
    in_refs = rest[:n_in]
    w_ref, o_ref = rest[n_in:]
    acc = None
    k0 = 0
    for r in in_refs:
        k = r.shape[-1]
        part = _dot(r[...], w_ref[k0:k0 + k, :])
        acc = part if acc is None else acc + part
        k0 += k
    o_ref[...] = x_ref[...] + mods_ref[0][2:3, :] * acc


def _out_proj_call(x, mods, rows_per_cond, parts, w_o, tm=512):
    r = x.shape[0]
    row = lambda i: (i, 0)
    in_specs = [
        pl.BlockSpec((tm, D_MODEL), row),
        pl.BlockSpec((1, N_MODS, D_MODEL), lambda i: ((i * tm) // rows_per_cond, 0, 0)),
    ]
    in_specs += [pl.BlockSpec((tm, p.shape[-1]), row) for p in parts]
    in_specs += [pl.BlockSpec(w_o.shape, lambda i: (0, 0))]
    return pl.pallas_call(
        functools.partial(_out_proj_kernel, len(parts)),
        grid=(r // tm,),
        in_specs=in_specs,
        out_specs=pl.BlockSpec((tm, D_MODEL), row),
        out_shape=jax.ShapeDtypeStruct((r, D_MODEL), F32),
        compiler_params=_cparams("out_proj", 1),
        name="out_proj",
    )(x, mods, *parts, w_o)


def _odd_in_kernel(tm, x_ref, mods_ref, g_ref, w_ref, lg_ref, lb_ref, ws_ref, bs_ref, o_ref):
    h = _modnorm(x_ref[...], g_ref[...], mods_ref[0], 0).astype(BF16)
    v = _gelu_tanh(_dot(h, w_ref[:, D_CMLP:]))
    v = _layernorm(v, lg_ref[...], lb_ref[...]).astype(BF16)
    u = _gelu_tanh(_dot(h, w_ref[:, :D_CMLP]))
    for ch in range(tm // CHUNK):
        rs = slice(ch * CHUNK, (ch + 1) * CHUNK)
        for g in range(CMLP_GROUPS):
            cs = slice(g * CMLP_GROUP_W, (g + 1) * CMLP_GROUP_W)
            mixed = _dot(ws_ref[g], v[rs, cs]) + bs_ref[:, cs]
            o_ref[rs, cs] = (u[rs, cs] * mixed).astype(BF16)


def _odd_in_call(x, mods, rows_per_cond, g_mix, w_in, ln_g, ln_b, w_s, b_s_full, tm=256):
    r = x.shape[0]
    const = lambda i: (0, 0)
    row = lambda i: (i, 0)
    return pl.pallas_call(
        functools.partial(_odd_in_kernel, tm),
        grid=(r // tm,),
        in_specs=[
            pl.BlockSpec((tm, D_MODEL), row),
            pl.BlockSpec((1, N_MODS, D_MODEL), lambda i: ((i * tm) // rows_per_cond, 0, 0)),
            pl.BlockSpec((1, D_MODEL), const),
            pl.BlockSpec(w_in.shape, const, pipeline_mode=pl.Buffered(1)),
            pl.BlockSpec((1, D_CMLP), const),
            pl.BlockSpec((1, D_CMLP), const),
            pl.BlockSpec(w_s.shape, lambda i: (0, 0, 0)),
            pl.BlockSpec((CHUNK, D_CMLP), const),
        ],
        out_specs=pl.BlockSpec((tm, D_CMLP), row),
        out_shape=jax.ShapeDtypeStruct((r, D_CMLP), BF16),
        compiler_params=_cparams("odd_in", 1),
        name="odd_in",
    )(x, mods, g_mix, w_in, ln_g, ln_b, w_s, b_s_full)


def _ffn_kernel(final, x_ref, mods_ref, g_ref, wg_ref, wu_ref, wd_ref, *rest):
    if final:
        fg_ref, o_ref, h_ref, gs_ref, r_ref = rest
    else:
        o_ref, h_ref, gs_ref, r_ref = rest
    j = pl.program_id(1)
    tm = x_ref.shape[0]
    lane_tiles = [slice(c * LANES, (c + 1) * LANES) for c in range(D_MODEL // LANES)]

    @pl.when(j == 0)
    def _():
        gs_ref[0:1, :] = g_ref[...] * (1.0 + mods_ref[0][4:5, :])
        gs_ref[1:2, :] = mods_ref[0][3:4, :]
        gs_ref[2:3, :] = mods_ref[0][5:6, :]
        _fill_rms_scale(tm, lambda rows: x_ref[rows, :], r_ref)

        def prologue(rows):
            r = r_ref[rows, :]
            for cs in lane_tiles:
                h_ref[rows, cs] = (x_ref[rows, cs] * r * gs_ref[0:1, cs] + gs_ref[1:2, cs]).astype(BF16)

        _row_loop(tm, NORM_ROWS, prologue, unroll=2)

    def gated_down_proj():
        h = h_ref[...]
        hid = (_silu(_dot(h, wg_ref[...].astype(BF16))) * _dot(h, wu_ref[...].astype(BF16))).astype(BF16)
        return gs_ref[2:3, :] * _dot(hid, wd_ref[...].astype(BF16))

    @pl.when(j == 0)
    def _():
        o_ref[...] = x_ref[...] + gated_down_proj()

    @pl.when(j > 0)
    def _():
        o_ref[...] += gated_down_proj()

    if final:
        @pl.when(j == pl.num_programs(1) - 1)
        def _():
            _fill_rms_scale(tm, lambda rows: o_ref[rows, :], r_ref)

            def final_norm(rows):
                r = r_ref[rows, :]
                for cs in lane_tiles:
                    o_ref[rows, cs] = o_ref[rows, cs] * r * fg_ref[:, cs]

            _row_loop(tm, NORM_ROWS, final_norm, unroll=2)


def _ffn_call(x, mods, rows_per_cond, g_ffn, layer, wg, wu, wd, final_g=None, tm=1024, tn=256):
    r = x.shape[0]
    final = final_g is not None
    const = lambda i, j: (0, 0)
    row = lambda i, j: (i, 0)
    in_specs = [
        pl.BlockSpec((tm, D_MODEL), row),
        pl.BlockSpec((1, N_MODS, D_MODEL), lambda i, j: ((i * tm) // rows_per_cond, 0, 0)),
        pl.BlockSpec((1, D_MODEL), const),
        pl.BlockSpec((None, D_MODEL, tn), lambda i, j: (layer, 0, j)),
        pl.BlockSpec((None, D_MODEL, tn), lambda i, j: (layer, 0, j)),
        pl.BlockSpec((None, tn, D_MODEL), lambda i, j: (layer, j, 0)),
    ]
    args = [x, mods, g_ffn, wg, wu, wd]
    if final:
        in_specs.append(pl.BlockSpec((1, D_MODEL), const))
        args.append(final_g)
    return pl.pallas_call(
        functools.partial(_ffn_kernel, final),
        grid=(r // tm, D_FF // tn),
        in_specs=in_specs,
        out_specs=pl.BlockSpec((tm, D_MODEL), row),
        out_shape=jax.ShapeDtypeStruct((r, D_MODEL), F32),
        scratch_shapes=[
            pltpu.VMEM((tm, D_MODEL), BF16),
            pltpu.VMEM((SUBLANES, D_MODEL), F32),
            pltpu.VMEM((tm, LANES), F32),
        ],
        compiler_params=_cparams("ffn", 2),
        name="ffn",
    )(*args)


def _rope_tables(n_tokens):
    n_freq = ROPE_DIM // 4
    lane = np.arange(ROPE_DIM)
    tok = np.arange(n_tokens)
    pos = np.where(lane[None, :] // (2 * n_freq) == 0, tok[:, None] // GRID_W, tok[:, None] % GRID_W).astype(np.float32)
    exponent = -np.arange(n_freq, dtype=np.float32) * np.float32(2.0) / np.float32(ROPE_DIM // 2)
    inv_freq = np.power(np.float32(ROPE_THETA), exponent).astype(np.float32)
    ang = pos * inv_freq[lane % n_freq][None, :]
    sign = np.where((lane // n_freq) % 2 == 0, -1.0, 1.0).astype(np.float32)
    return np.cos(ang).astype(np.float32), (np.sin(ang) * sign[None, :]).astype(np.float32)


def _pair_swap(w):
    perm = jnp.arange(ROPE_DIM) ^ (ROPE_DIM // 4)
    return w[..., perm]


def kernel(x_prompt, x_sample, cache_ckv, cache_kpe, c, c_ctx, mod_w, mod_b, norm_mix_g, norm_ffn_g, ffn_w_gate, ffn_w_up, ffn_w_down, ev_w_in, ev_conv_w, ev_conv_b, ev_conv_ln_g, ev_conv_ln_b, ev_q_norm_g, ev_w_qb, ev_kv_norm_g, ev_w_kvb, ev_w_o, od_w_in, od_ln_g, od_ln_b, od_w_s, od_b_s, od_w_o, final_norm_g):
    bp, tp, _ = x_prompt.shape
    bs, ts, _ = x_sample.shape
    xp = x_prompt.reshape(bp * tp, D_MODEL)
    xs = x_sample.reshape(bs * ts, D_MODEL)

    cond8 = jnp.concatenate([c_ctx[None, :], c, jnp.zeros((N_MODS - 1 - bs, D_MODEL), F32)], axis=0)
    mods = _mods_call(cond8, mod_w, mod_b)
    mods = mods[:, :1 + bs].reshape(DEPTH, 1 + bs, 6, D_MODEL)
    mods = jnp.pad(mods, ((0, 0), (0, 0), (0, N_MODS - 6), (0, 0)))

    cos_k, sin_k = _rope_tables(ts)
    zpad = np.zeros((ts, LANES - ROPE_DIM), np.float32)
    cos_q = np.concatenate([cos_k, zpad], axis=-1)
    sin_q = np.concatenate([sin_k, zpad], axis=-1)

    streams = [
        dict(x=xp, batch=bp, t=tp, rows=bp * tp, conds=slice(0, 1), rope=False),
        dict(x=xs, batch=bs, t=ts, rows=ts, conds=slice(1, 1 + bs), rope=True),
    ]
    state = {}
    for l in range(DEPTH):
        g_mix = norm_mix_g[l][None, :]
        g_ffn = norm_ffn_g[l][None, :]
        last = l == DEPTH - 1
        if l % 2 == 0:
            e = l // 2
            w_in = ev_w_in[e]
            w1 = w_in.astype(BF16)
            wk2 = jnp.concatenate([w_in[:, O_KV:], _pair_swap(w_in[:, O_KV:])], axis=1).astype(BF16)
            wq3 = ev_w_qb[e].reshape(Q_RANK, MLA_HEADS, QK_DIM)
            wq = jnp.pad(wq3, ((0, 0), (0, 0), (0, HEAD_PAD - QK_DIM))).reshape(Q_RANK, -1).astype(BF16)
            wqs = jnp.pad(_pair_swap(wq3[:, :, NOPE_DIM:]), ((0, 0), (0, 0), (0, LANES - ROPE_DIM)))
            wqs = wqs.reshape(Q_RANK, -1).astype(BF16)
            wkv = ev_w_kvb[e].astype(BF16)
            wkv_t = ev_w_kvb[e].T.astype(BF16)
            w_o = ev_w_o[e].astype(BF16)
            conv_w = ev_conv_w[e].reshape(CONV_WIDTH, D_CONV)
            for s in streams:
                m = mods[l, s["conds"]]
                rope_args = (wqs, cos_q, sin_q, cos_k, sin_k) if s["rope"] else None
                a, q, ckv, kpe = _even_in_call(s["x"], m, s["rows"], g_mix, w1, wk2, ev_q_norm_g[e][None, :],
                                               ev_kv_norm_g[e][None, :], wq, rope_args)
                a = _conv_call(a, s["batch"], s["t"], conv_w, ev_conv_b[e][None, :],
                               ev_conv_ln_g[e][None, :], ev_conv_ln_b[e][None, :], tt=min(s["t"], 256))
                if s["rope"]:
                    att = _attn_call(q, ckv, kpe, wkv, wkv_t, s["batch"], s["t"],
                                     (cache_ckv[:, e], cache_kpe[:, e]), tq=2048)
                else:
                    att = _attn_seq_call(q, ckv, kpe, wkv, s["batch"], s["t"])
                    state.setdefault("ckv", []).append(ckv.reshape(bp, tp, KV_RANK))
                    state.setdefault("kpe", []).append(kpe.reshape(bp, tp, ROPE_DIM))
                s["x"] = _out_proj_call(s["x"], m, s["rows"], [a, att], w_o)
        else:
            o = l // 2
            w_in = od_w_in[o].astype(BF16)
            w_o = od_w_o[o].astype(BF16)
            w_s = od_w_s[o].astype(BF16)
            b_s_full = jnp.repeat(od_b_s[o].T, CMLP_GROUP_W, axis=1)
            for s in streams:
                m = mods[l, s["conds"]]
                p = _odd_in_call(s["x"], m, s["rows"], g_mix, w_in, od_ln_g[o][None, :], od_ln_b[o][None, :],
                                 w_s, b_s_full)
                s["x"] = _out_proj_call(s["x"], m, s["rows"], [p], w_o)
        for s in streams:
            m = mods[l, s["conds"]]
            s["x"] = _ffn_call(s["x"], m, s["rows"], g_ffn, l, ffn_w_gate, ffn_w_up, ffn_w_down,
                               final_g=final_norm_g[None, :] if last else None)

    y_prompt = streams[0]["x"].reshape(bp, tp, D_MODEL)
    y_sample = streams[1]["x"].reshape(bs, ts, D_MODEL)
    state_ckv = jnp.stack(state["ckv"], axis=1)
    state_kpe = jnp.stack(state["kpe"], axis=1)
    return (y_prompt, y_sample, state_ckv, state_kpe)
```
